```python
import math
import jax
import jax.numpy as jnp
from jax import lax
import numpy as np

D_MODEL = 4096
BATCH = 4
SEQ = 2048
DEPTH = 4
DEC_BATCH = 128
DEC_SEQ = 1
PAST_LEN = 16384
PAGE_SIZE = 128

N_MIXERS = 3
S5_GROUP = 16
S5_GROUPS = D_MODEL // S5_GROUP
S5_STATE = 64
S5_CHUNK = 128
CONV_WIDTH = 3
HG_DK = 128
HG_HEADS = D_MODEL // HG_DK
HG_DV = D_MODEL // HG_HEADS
HG_CHUNK = 32
D_FF = -(-(8 * D_MODEL) // (3 * 256)) * 256
N_S5_LAYERS = (DEPTH + N_MIXERS - 1) // N_MIXERS
N_CONV_LAYERS = (DEPTH + N_MIXERS - 2) // N_MIXERS
N_HGRN_LAYERS = (DEPTH + N_MIXERS - 3) // N_MIXERS
RMS_EPS = 1e-6

kernel_name = 'hybrid_s5_shortconv_hgrn2_step'


def rmsnorm(x, w):
    xf = x.astype(jnp.float32)
    y = xf * lax.rsqrt(jnp.mean(xf * xf, axis=-1, keepdims=True) + RMS_EPS)
    return (y * w.astype(jnp.float32)).astype(x.dtype)


def swiglu(x, w_in, w_out):
    gu = x @ w_in
    return (jax.nn.silu(gu[..., :D_FF]) * gu[..., D_FF:]) @ w_out


def _complex_combine(e1, e2):
    a1r, a1i, b1r, b1i = e1
    a2r, a2i, b2r, b2i = e2
    return (a2r * a1r - a2i * a1i, a2r * a1i + a2i * a1r,
            a2r * b1r - a2i * b1i + b2r, a2r * b1i + a2i * b1r + b2i)


def s5_mixer(u, h_re, h_im, a_re, a_im, log_dt, b_re, b_im, c_re, c_im, d_skip, w_glu):
    n, t, _ = u.shape
    f32 = jnp.float32
    ar = a_re.astype(f32)
    ai = a_im.astype(f32)
    dt = jnp.exp(log_dt.astype(f32))[:, None]
    mag = jnp.exp(ar * dt)
    lam_re = mag * jnp.cos(ai * dt)
    lam_im = mag * jnp.sin(ai * dt)
    den = ar * ar + ai * ai
    co_re = ((lam_re - 1.0) * ar + lam_im * ai) / den
    co_im = (lam_im * ar - (lam_re - 1.0) * ai) / den
    br = b_re.astype(f32)
    bi = b_im.astype(f32)
    bb_re = co_re[..., None] * br - co_im[..., None] * bi
    bb_im = co_re[..., None] * bi + co_im[..., None] * br
    cr = c_re.astype(f32)
    ci = c_im.astype(f32)
    dd = d_skip.astype(f32).reshape(S5_GROUPS, S5_GROUP)
    L = S5_CHUNK if t % S5_CHUNK == 0 else t
    nc = t // L
    uc_all = u.astype(f32).reshape(n, nc, L, S5_GROUPS, S5_GROUP).swapaxes(0, 1)

    def chunk_step(carry, uc):
        hr, hi = carry
        bu_re = jnp.einsum('nlgc,gpc->nlgp', uc, bb_re)
        bu_im = jnp.einsum('nlgc,gpc->nlgp', uc, bb_im)
        pw_re, pw_im, acc_re, acc_im = lax.associative_scan(
            _complex_combine,
            (jnp.broadcast_to(lam_re, bu_re.shape), jnp.broadcast_to(lam_im, bu_im.shape), bu_re, bu_im),
            axis=1)
        xr = pw_re * hr[:, None] - pw_im * hi[:, None] + acc_re
        xi = pw_re * hi[:, None] + pw_im * hr[:, None] + acc_im
        y = (jnp.einsum('nlgp,gcp->nlgc', xr, cr) - jnp.einsum('nlgp,gcp->nlgc', xi, ci)
             + dd * uc)
        return (xr[:, -1], xi[:, -1]), y

    (hr, hi), y = lax.scan(chunk_step, (h_re.astype(f32), h_im.astype(f32)), uc_all)
    y = y.swapaxes(0, 1).reshape(n, t, D_MODEL)
    z = jax.nn.gelu(y).astype(u.dtype)
    ab = z @ w_glu
    out = ab[..., :D_MODEL] * jax.nn.sigmoid(ab[..., D_MODEL:])
    return out, hr.astype(h_re.dtype), hi.astype(h_im.dtype)


def short_conv_mixer(x, buf, w_in, conv_w, w_out):
    t = x.shape[1]
    gb, gc, v = jnp.split(x @ w_in, 3, axis=-1)
    pre = gc * v
    xpad = jnp.concatenate([buf.astype(pre.dtype), pre], axis=1)
    conv = conv_w[0] * xpad[:, 0:t]
    for k in range(1, CONV_WIDTH):
        conv = conv + conv_w[k] * xpad[:, k:k + t]
    return (gb * conv) @ w_out, xpad[:, t:]


def hgrn2_mixer(x, s0, w_in, lb, gnorm, w_out):
    n, t, _ = x.shape
    f32 = jnp.float32
    q, fpre, iv, g = jnp.split(x @ w_in, 4, axis=-1)
    q = jax.nn.silu(q.astype(f32)).reshape(n, t, HG_HEADS, HG_DK)
    lbh = lb.astype(f32).reshape(HG_HEADS, HG_DK)
    f = lbh + (1.0 - lbh) * jax.nn.sigmoid(fpre.astype(f32).reshape(n, t, HG_HEADS, HG_DK))
    logf = jnp.log(f)
    k = 1.0 - f
    iv = iv.astype(f32).reshape(n, t, HG_HEADS, HG_DV)
    L = HG_CHUNK if t % HG_CHUNK == 0 else t
    nc = t // L

    def to_chunks(a):
        return a.reshape(n, nc, L, a.shape[2], a.shape[3]).swapaxes(0, 1)

    causal = jnp.tril(jnp.ones((L, L), dtype=bool))[None, :, :, None, None]

    def chunk_step(S, inp):
        qc, kc, lc, vc = inp
        b = jnp.cumsum(lc, axis=1)
        decay = jnp.exp(jnp.where(causal, b[:, :, None] - b[:, None, :], -jnp.inf))
        att = jnp.einsum('nlhk,nlshk,nshk->nhls', qc, decay, kc)
        o = (jnp.einsum('nlhk,nhkv->nlhv', qc * jnp.exp(b), S)
             + jnp.einsum('nhls,nshv->nlhv', att, vc))
        bl = b[:, -1]
        S = (jnp.exp(bl)[..., None] * S
             + jnp.einsum('nshk,nshv->nhkv', kc * jnp.exp(bl[:, None] - b), vc))
        return S, o

    S, o = lax.scan(chunk_step, s0.astype(f32),
                    (to_chunks(q), to_chunks(k), to_chunks(logf), to_chunks(iv)))
    o = o.swapaxes(0, 1).reshape(n, t, HG_HEADS, HG_DV)
    o = o * lax.rsqrt(jnp.mean(o * o, axis=-1, keepdims=True) + RMS_EPS)
    o = o * gnorm.astype(f32).reshape(HG_HEADS, HG_DV)
    o = o.reshape(n, t, D_MODEL) * jax.nn.sigmoid(g.astype(f32))
    return o.astype(x.dtype) @ w_out, S.astype(s0.dtype)


def trunk(x, s5_re, s5_im, conv_buf, hg_state, p):
    lb_cum = jnp.cumsum(jax.nn.softmax(p['hgrn_lb_logits'].astype(jnp.float32), axis=0), axis=0)
    out_re, out_im, out_conv, out_hg = [], [], [], []
    h = x
    for layer in range(DEPTH):
        kind, j = layer % N_MIXERS, layer // N_MIXERS
        u = rmsnorm(h, p['norm_mix'][layer])
        if kind == 0:
            m, nr, ni = s5_mixer(u, s5_re[j], s5_im[j], p['s5_a_re'][j], p['s5_a_im'][j],
                                 p['s5_log_dt'][j], p['s5_b_re'][j], p['s5_b_im'][j],
                                 p['s5_c_re'][j], p['s5_c_im'][j], p['s5_d'][j], p['s5_w_glu'][j])
            out_re.append(nr)
            out_im.append(ni)
        elif kind == 1:
            m, nb = short_conv_mixer(u, conv_buf[j], p['conv_w_in'][j], p['conv_w'][j],
                                     p['conv_w_out'][j])
            out_conv.append(nb)
        else:
            lb = lb_cum[layer] - lb_cum[0]
            m, ns = hgrn2_mixer(u, hg_state[j], p['hgrn_w_in'][j], lb, p['hgrn_gnorm'][j],
                                p['hgrn_w_out'][j])
            out_hg.append(ns)
        h = h + m
        h = h + swiglu(rmsnorm(h, p['norm_ffn'][layer]), p['ffn_w_in'][layer], p['ffn_w_out'][layer])
    return (rmsnorm(h, p['norm_final']), jnp.stack(out_re), jnp.stack(out_im),
            jnp.stack(out_conv), jnp.stack(out_hg))


def setup_inputs(seed: int = 0) -> dict:
    key = jax.random.key(seed)
    ks = jax.random.split(key, 32)
    f32 = jnp.float32

    def nrm(k, shape, s):
        return s * jax.random.normal(k, shape, f32)

    n_idx = jnp.arange(S5_STATE, dtype=f32)
    return {
        'x_prompt': nrm(ks[0], (BATCH, SEQ, D_MODEL), 1.0),
        'x_sample': nrm(ks[1], (DEC_BATCH, DEC_SEQ, D_MODEL), 1.0),
        'state_s5_re': nrm(ks[2], (N_S5_LAYERS, DEC_BATCH, S5_GROUPS, S5_STATE), 0.5),
        'state_s5_im': nrm(ks[3], (N_S5_LAYERS, DEC_BATCH, S5_GROUPS, S5_STATE), 0.5),
        'state_conv': nrm(ks[4], (N_CONV_LAYERS, DEC_BATCH, CONV_WIDTH - 1, D_MODEL), 1.0),
        'state_hgrn': nrm(ks[5], (N_HGRN_LAYERS, DEC_BATCH, HG_HEADS, HG_DK, HG_DV), 0.5),
        'norm_mix': 1.0 + nrm(ks[6], (DEPTH, D_MODEL), 0.02),
        'norm_ffn': 1.0 + nrm(ks[7], (DEPTH, D_MODEL), 0.02),
        'norm_final': 1.0 + nrm(ks[8], (D_MODEL,), 0.02),
        's5_a_re': -0.5 + nrm(ks[9], (N_S5_LAYERS, S5_GROUPS, S5_STATE), 0.01),
        's5_a_im': math.pi * n_idx + nrm(ks[10], (N_S5_LAYERS, S5_GROUPS, S5_STATE), 0.01),
        's5_log_dt': jax.random.uniform(ks[11], (N_S5_LAYERS, S5_GROUPS), f32,
                                        math.log(1e-3), math.log(1e-1)),
        's5_b_re': nrm(ks[12], (N_S5_LAYERS, S5_GROUPS, S5_STATE, S5_GROUP), S5_GROUP ** -0.5),
        's5_b_im': nrm(ks[13], (N_S5_LAYERS, S5_GROUPS, S5_STATE, S5_GROUP), S5_GROUP ** -0.5),
        's5_c_re': nrm(ks[14], (N_S5_LAYERS, S5_GROUPS, S5_GROUP, S5_STATE), (2 * S5_STATE) ** -0.5),
        's5_c_im': nrm(ks[15], (N_S5_LAYERS, S5_GROUPS, S5_GROUP, S5_STATE), (2 * S5_STATE) ** -0.5),
        's5_d': nrm(ks[16], (N_S5_LAYERS, D_MODEL), 1.0),
        's5_w_glu': nrm(ks[17], (N_S5_LAYERS, D_MODEL, 2 * D_MODEL), D_MODEL ** -0.5),
        'conv_w_in': nrm(ks[18], (N_CONV_LAYERS, D_MODEL, 3 * D_MODEL), D_MODEL ** -0.5),
        'conv_w': nrm(ks[19], (N_CONV_LAYERS, CONV_WIDTH, D_MODEL), CONV_WIDTH ** -0.5),
        'conv_w_out': nrm(ks[20], (N_CONV_LAYERS, D_MODEL, D_MODEL), D_MODEL ** -0.5),
        'hgrn_w_in': nrm(ks[21], (N_HGRN_LAYERS, D_MODEL, 4 * D_MODEL), D_MODEL ** -0.5),
        'hgrn_lb_logits': nrm(ks[22], (DEPTH, HG_HEADS * HG_DK), 0.1),
        'hgrn_gnorm': 1.0 + nrm(ks[23], (N_HGRN_LAYERS, D_MODEL), 0.02),
        'hgrn_w_out': nrm(ks[24], (N_HGRN_LAYERS, D_MODEL, D_MODEL), D_MODEL ** -0.5),
        'ffn_w_in': nrm(ks[25], (DEPTH, D_MODEL, 2 * D_FF), D_MODEL ** -0.5),
        'ffn_w_out': nrm(ks[26], (DEPTH, D_FF, D_MODEL), D_FF ** -0.5),
    }


def reference(x_prompt, x_sample, state_s5_re, state_s5_im, state_conv, state_hgrn,
              norm_mix, norm_ffn, norm_final,
              s5_a_re, s5_a_im, s5_log_dt, s5_b_re, s5_b_im, s5_c_re, s5_c_im, s5_d, s5_w_glu,
              conv_w_in, conv_w, conv_w_out,
              hgrn_w_in, hgrn_lb_logits, hgrn_gnorm, hgrn_w_out,
              ffn_w_in, ffn_w_out):
    params = dict(norm_mix=norm_mix, norm_ffn=norm_ffn, norm_final=norm_final,
                  s5_a_re=s5_a_re, s5_a_im=s5_a_im, s5_log_dt=s5_log_dt,
                  s5_b_re=s5_b_re, s5_b_im=s5_b_im, s5_c_re=s5_c_re, s5_c_im=s5_c_im,
                  s5_d=s5_d, s5_w_glu=s5_w_glu,
                  conv_w_in=conv_w_in, conv_w=conv_w, conv_w_out=conv_w_out,
                  hgrn_w_in=hgrn_w_in, hgrn_lb_logits=hgrn_lb_logits, hgrn_gnorm=hgrn_gnorm,
                  hgrn_w_out=hgrn_w_out, ffn_w_in=ffn_w_in, ffn_w_out=ffn_w_out)
    nb = x_prompt.shape[0]
    dt = x_prompt.dtype
    z_re = jnp.zeros((N_S5_LAYERS, nb, S5_GROUPS, S5_STATE), dt)
    z_im = jnp.zeros((N_S5_LAYERS, nb, S5_GROUPS, S5_STATE), dt)
    z_conv = jnp.zeros((N_CONV_LAYERS, nb, CONV_WIDTH - 1, D_MODEL), dt)
    z_hg = jnp.zeros((N_HGRN_LAYERS, nb, HG_HEADS, HG_DK, HG_DV), dt)
    y_prompt, p_re, p_im, p_conv, p_hg = trunk(x_prompt, z_re, z_im, z_conv, z_hg, params)
    y_sample, s_re, s_im, s_conv, s_hg = trunk(x_sample, state_s5_re, state_s5_im, state_conv,
                                               state_hgrn, params)
    return (y_prompt, y_sample, p_re, p_im, p_conv, p_hg, s_re, s_im, s_conv, s_hg)
```

```python
import functools
import math

import jax
import jax.numpy as jnp
from jax import lax
from jax.experimental import pallas as pl
from jax.experimental.pallas import tpu as pltpu

F32 = jnp.float32
BF16 = jnp.bfloat16

RMS_EPS = 1e-6
N_MIXERS = 3
S5_GROUP = 16
S5_STATE = 64
S5_CHUNK = 16
HG_DK = 128
HG_CHUNK = 64
LANES = 128
S5_GPL = LANES // S5_GROUP
S5_HALF = S5_GPL * S5_STATE
VMEM_LIMIT = 56 * 1024 * 1024


def _params(sem):
    return pltpu.CompilerParams(dimension_semantics=sem, vmem_limit_bytes=VMEM_LIMIT)


def _pick(dim, candidates):
    for c in candidates:
        if dim % c == 0:
            return c
    return dim


def _rms(x, w):
    return x * lax.rsqrt(jnp.mean(x * x, axis=-1, keepdims=True) + RMS_EPS) * w


def _rmsnorm_kernel(x_ref, w_ref, o_ref):
    o_ref[...] = _rms(x_ref[...], w_ref[...]).astype(o_ref.dtype)


def rmsnorm_rows(x, w, row_start, n_rows, out_dtype=F32):
    d = x.shape[1]
    tm = _pick(math.gcd(row_start, n_rows) if row_start else n_rows, (128, 64, 32, 16, 8))
    off = row_start // tm
    return pl.pallas_call(
        _rmsnorm_kernel,
        grid=(n_rows // tm,),
        in_specs=[pl.BlockSpec((tm, d), lambda i: (i + off, 0)),
                  pl.BlockSpec((1, d), lambda i: (0, 0))],
        out_specs=pl.BlockSpec((tm, d), lambda i: (i, 0)),
        out_shape=jax.ShapeDtypeStruct((n_rows, d), out_dtype),
        compiler_params=_params(("arbitrary",)),
        name="rmsnorm",
    )(x, w.reshape(1, d))


def _epi_plain(accs, res):
    return accs[0]


def _epi_res(accs, res):
    return res + accs[0]


def _epi_glu_res(accs, res):
    return res + accs[0] * jax.nn.sigmoid(accs[1])


def _epi_swiglu(accs, res):
    return jax.nn.silu(accs[0]) * accs[1]


def _dense_kernel(*refs, n_w, prologue, has_res, epi):
    it = iter(refs)
    x_ref = next(it)
    nw_ref = next(it) if prologue == "norm" else None
    w_refs = [next(it) for _ in range(n_w)]
    res_ref = next(it) if has_res else None
    o_ref = next(it)
    xs_ref = next(it) if prologue != "none" else None

    if prologue != "none":
        @pl.when(pl.program_id(1) == 0)
        def _():
            tm = x_ref.shape[0]
            rc = _pick(tm, (128, 64, 32, 16, 8))
            for r in range(0, tm, rc):
                x = x_ref[r:r + rc, :]
                if prologue == "norm":
                    x = _rms(x, nw_ref[...])
                xs_ref[r:r + rc, :] = x.astype(BF16)
        xb = xs_ref[...]
    else:
        xb = x_ref[...]
    accs = [jnp.dot(xb, w[...], preferred_element_type=F32) for w in w_refs]
    res = res_ref[...] if has_res else None
    o_ref[...] = epi(accs, res).astype(o_ref.dtype)


def dense(x, ws, *, epi, out_dtype, norm_w=None, res=None, tn=512, name="dense"):
    m, k = x.shape
    n = ws[0].shape[1]
    tm = _pick(m, (640, 512, 256, 128, 64, 32, 16, 8))
    tn = _pick(n, (tn, 256, 128))
    if norm_w is not None:
        prologue = "norm"
    elif x.dtype != BF16:
        prologue = "cast"
    else:
        prologue = "none"
    in_specs = [pl.BlockSpec((tm, k), lambda i, j: (i, 0))]
    args = [x]
    if prologue == "norm":
        in_specs.append(pl.BlockSpec((1, k), lambda i, j: (0, 0)))
        args.append(norm_w.reshape(1, k))
    for w in ws:
        in_specs.append(pl.BlockSpec((k, tn), lambda i, j: (0, j)))
        args.append(w)
    if res is not None:
        in_specs.append(pl.BlockSpec((tm, tn), lambda i, j: (i, j)))
        args.append(res)
    scratch = [pltpu.VMEM((tm, k), BF16)] if prologue != "none" else []
    kern = functools.partial(_dense_kernel, n_w=len(ws), prologue=prologue,
                             has_res=res is not None, epi=epi)
    return pl.pallas_call(
        kern,
        grid=(m // tm, n // tn),
        in_specs=in_specs,
        out_specs=pl.BlockSpec((tm, tn), lambda i, j: (i, j)),
        out_shape=jax.ShapeDtypeStruct((m, n), out_dtype),
        scratch_shapes=scratch,
        compiler_params=_params(("arbitrary", "arbitrary")),
        name=name,
    )(*args)


def _s5_kernel(u_ref, sre_ref, sim_ref, ar_ref, ai_ref, ldt_ref, br_ref, bi_ref, cr_ref, ci_ref,
               d_ref, z_ref, ore_ref, oim_ref, tmat, qmat, pmat, binp, xst, *, L, nc, rows):
    H = S5_HALF

    ar = ar_ref[...]
    ai = ai_ref[...]
    dt = jnp.exp(ldt_ref[...])
    mag = jnp.exp(ar * dt)
    lr = mag * jnp.cos(ai * dt)
    li = mag * jnp.sin(ai * dt)
    den = ar * ar + ai * ai
    cor = ((lr - 1.0) * ar + li * ai) / den
    coi = (li * ar - (lr - 1.0) * ai) / den

    pows = [(jnp.ones_like(lr), jnp.zeros_like(lr))]
    for _ in range(L):
        pr, pi = pows[-1]
        pows.append((pr * lr - pi * li, pr * li + pi * lr))

    @pl.when(pl.program_id(1) == 0)
    def _build():
        row_g = lax.broadcasted_iota(jnp.int32, (LANES, H), 0) // S5_GROUP
        lane_g = lax.broadcasted_iota(jnp.int32, (LANES, H), 1) // S5_STATE
        diag = row_g == lane_g
        br = jnp.where(diag, br_ref[...], 0.0)
        bi = jnp.where(diag, bi_ref[...], 0.0)
        cr = jnp.where(diag, cr_ref[...], 0.0)
        ci = jnp.where(diag, ci_ref[...], 0.0)
        a_r = br * cor - bi * coi
        a_i = br * coi + bi * cor
        a_cat = jnp.concatenate([a_r, a_i], axis=1).astype(BF16)
        zero_blk = jnp.zeros((LANES, LANES), BF16)
        for k in range(L + 1):
            pr, pi = pows[k]
            w_k = jnp.concatenate([cr * pr - ci * pi, -(cr * pi + ci * pr)], axis=1).astype(BF16)
            if k >= 1:
                pmat[(k - 1) * LANES:k * LANES, :] = w_k
            if k < L:
                d_k = lax.dot_general(a_cat, w_k, (((1,), (1,)), ((), ())),
                                      preferred_element_type=F32).astype(BF16)
                for s in range(L - k):
                    t = s + k
                    tmat[s * LANES:(s + 1) * LANES, t * LANES:(t + 1) * LANES] = d_k
                s = L - 1 - k
                qmat[s * LANES:(s + 1) * LANES, :] = jnp.concatenate(
                    [a_r * pr - a_i * pi, a_r * pi + a_i * pr], axis=1).astype(BF16)
        for s in range(L):
            for t in range(s):
                tmat[s * LANES:(s + 1) * LANES, t * LANES:(t + 1) * LANES] = zero_blk

    if L > 1:
        slabs = [u_ref[:, s, :] for s in range(L)]
    else:
        slabs = [u_ref[...]]
    xcat = jnp.concatenate([s.astype(BF16) for s in slabs], axis=1) if L > 1 else slabs[0].astype(BF16)
    y = jnp.dot(xcat, tmat[...], preferred_element_type=F32)
    b_in = jnp.dot(xcat, qmat[...], preferred_element_type=F32)
    lLr, lLi = pows[L]

    if nc == 1:
        xr = sre_ref[...]
        xi = sim_ref[...]
        xstart = jnp.concatenate([xr, xi], axis=1)
        ore_ref[...] = xr * lLr - xi * lLi + b_in[:, :H]
        oim_ref[...] = xr * lLi + xi * lLr + b_in[:, H:]
    else:
        binp[...] = b_in

        def step(c, carry):
            xr, xi = carry
            xst[pl.ds(c, 1), 0:H] = xr
            xst[pl.ds(c, 1), H:2 * H] = xi
            b_r = binp[pl.ds(c, 1), 0:H]
            b_i = binp[pl.ds(c, 1), H:2 * H]
            return (xr * lLr - xi * lLi + b_r, xr * lLi + xi * lLr + b_i)

        xr, xi = lax.fori_loop(0, nc, step, (sre_ref[...], sim_ref[...]))
        ore_ref[...] = xr
        oim_ref[...] = xi
        xstart = xst[...]
    y = y + lax.dot_general(xstart.astype(BF16), pmat[...], (((1,), (1,)), ((), ())),
                            preferred_element_type=F32)
    dsk = d_ref[...]
    for t in range(L):
        yt = y[:, t * LANES:(t + 1) * LANES] + dsk * slabs[t]
        zt = jax.nn.gelu(yt)
        if L > 1:
            z_ref[:, t, :] = zt
        else:
            z_ref[...] = zt


def _s5_lane_rows(a):
    g, p = a.shape
    return a.reshape(g // S5_GPL, 1, S5_GPL * p)


def _s5_tiles(a):
    g, c, p = a.shape
    return jnp.tile(a.reshape(g // S5_GPL, S5_GPL * c, p), (1, 1, S5_GPL))


def s5_core(u, row_start, n_seq, t_len, st_re, st_im, a_re, a_im, log_dt, b_re, b_im, c_re, c_im, d_skip):
    m, d = u.shape
    g, p = a_re.shape
    nlg = d // LANES
    H = S5_HALF
    L = S5_CHUNK if t_len % S5_CHUNK == 0 else 1
    assert L > 1 or t_len == 1
    nc = t_len // L
    lane_args = [_s5_lane_rows(a_re), _s5_lane_rows(a_im),
                 _s5_lane_rows(jnp.broadcast_to(log_dt[:, None], (g, p)))]
    tile_args = [_s5_tiles(jnp.swapaxes(b_re, 1, 2)), _s5_tiles(jnp.swapaxes(b_im, 1, 2)),
                 _s5_tiles(c_re), _s5_tiles(c_im)]
    d_arg = d_skip.reshape(nlg, 1, LANES)
    lane_spec = pl.BlockSpec((None, 1, H), lambda j, n: (j, 0, 0))
    tile_spec = pl.BlockSpec((None, LANES, H), lambda j, n: (j, 0, 0))
    d_spec = pl.BlockSpec((None, 1, LANES), lambda j, n: (j, 0, 0))
    if nc == 1:
        rows = n_seq
        grid = (nlg, 1)
        rb = row_start // rows
        u_arg = u
        u_spec = pl.BlockSpec((rows, LANES), lambda j, n: (rb, j))
        z_shape = jax.ShapeDtypeStruct((rows, d), F32)
        z_spec = pl.BlockSpec((rows, LANES), lambda j, n: (0, j))
        st_args = [st_re, st_im]
        st_spec = pl.BlockSpec((rows, H), lambda j, n: (0, j))
        st_shape = jax.ShapeDtypeStruct((n_seq, g * p), F32)
    else:
        rows = nc
        grid = (nlg, n_seq)
        rb = row_start // t_len
        u_arg = u.reshape(m // L, L, d)
        u_spec = pl.BlockSpec((rows, L, LANES), lambda j, n: (rb + n, 0, j))
        z_shape = jax.ShapeDtypeStruct((n_seq * nc, L, d), F32)
        z_spec = pl.BlockSpec((rows, L, LANES), lambda j, n: (n, 0, j))
        st_args = [st_re.reshape(n_seq, 1, g * p), st_im.reshape(n_seq, 1, g * p)]
        st_spec = pl.BlockSpec((None, 1, H), lambda j, n: (n, 0, j))
        st_shape = jax.ShapeDtypeStruct((n_seq, 1, g * p), F32)
    kern = functools.partial(_s5_kernel, L=L, nc=nc, rows=rows)
    z, o_re, o_im = pl.pallas_call(
        kern,
        grid=grid,
        in_specs=[u_spec, st_spec, st_spec] + [lane_spec] * 3 + [tile_spec] * 4 + [d_spec],
        out_specs=[z_spec, st_spec, st_spec],
        out_shape=[z_shape, st_shape, st_shape],
        scratch_shapes=[pltpu.VMEM((L * LANES, L * LANES), BF16),
                        pltpu.VMEM((L * LANES, 2 * H), BF16),
                        pltpu.VMEM((L * LANES, 2 * H), BF16),
                        pltpu.VMEM((rows, 2 * H), F32),
                        pltpu.VMEM((rows, 2 * H), F32)],
        compiler_params=_params(("arbitrary", "arbitrary")),
        name="s5_core",
    )(u_arg, *st_args, *lane_args, *tile_args, d_arg)
    return z.reshape(n_seq * t_len, d), o_re.reshape(n_seq, g * p), o_im.reshape(n_seq, g * p)


def _conv_prompt_kernel(gb_ref, gc_ref, v_ref, buf_ref, cw_ref, z_ref, nb_ref):
    pre = gc_ref[...] * v_ref[...]
    t = pre.shape[0]
    buf = buf_ref[...]
    row = lax.broadcasted_iota(jnp.int32, pre.shape, 0)
    sh1 = jnp.where(row == 0, buf[1:2, :], pltpu.roll(pre, 1, 0))
    sh2 = pltpu.roll(pre, 2, 0)
    sh2 = jnp.where(row == 0, buf[0:1, :], jnp.where(row == 1, buf[1:2, :], sh2))
    cw = cw_ref[...]
    conv = cw[0:1, :] * sh2 + cw[1:2, :] * sh1 + cw[2:3, :] * pre
    z_ref[...] = (gb_ref[...] * conv).astype(z_ref.dtype)
    nb_ref[...] = pre[t - 2:t, :]


def conv_prompt(proj, n_seq, t_len, buf, conv_w):
    d = proj.shape[1] // 3
    tc = _pick(d, (256, 128))
    ncb = d // tc
    return pl.pallas_call(
        _conv_prompt_kernel,
        grid=(n_seq, ncb),
        in_specs=[pl.BlockSpec((t_len, tc), lambda n, j: (n, j)),
                  pl.BlockSpec((t_len, tc), lambda n, j: (n, ncb + j)),
                  pl.BlockSpec((t_len, tc), lambda n, j: (n, 2 * ncb + j)),
                  pl.BlockSpec((None, 2, tc), lambda n, j: (n, 0, j)),
                  pl.BlockSpec((3, tc), lambda n, j: (0, j))],
        out_specs=[pl.BlockSpec((t_len, tc), lambda n, j: (n, j)),
                   pl.BlockSpec((None, 2, tc), lambda n, j: (n, 0, j))],
        out_shape=[jax.ShapeDtypeStruct((n_seq * t_len, d), BF16),
                   jax.ShapeDtypeStruct((n_seq, 2, d), F32)],
        compiler_params=_params(("arbitrary", "arbitrary")),
        name="conv_prompt",
    )(proj, proj, proj, buf, conv_w)


def _conv_step_kernel(gb_ref, gc_ref, v_ref, b0_ref, b1_ref, cw_ref, z_ref, n0_ref, n1_ref):
    pre = gc_ref[...] * v_ref[...]
    b0 = b0_ref[...]
    b1 = b1_ref[...]
    cw = cw_ref[...]
    conv = cw[0:1, :] * b0 + cw[1:2, :] * b1 + cw[2:3, :] * pre
    z_ref[...] = (gb_ref[...] * conv).astype(z_ref.dtype)
    n0_ref[...] = b1
    n1_ref[...] = pre


def conv_step(proj, row_start, n_seq, buf, conv_w):
    d = proj.shape[1] // 3
    tc = _pick(d, (512, 256, 128))
    ncb = d // tc
    rb = row_start // n_seq
    buf2 = buf.reshape(n_seq, 2 * d)
    row_spec = lambda off: pl.BlockSpec((n_seq, tc), lambda j: (rb, off * ncb + j))
    b_spec = lambda off: pl.BlockSpec((n_seq, tc), lambda j: (0, off * ncb + j))
    o_spec = pl.BlockSpec((n_seq, tc), lambda j: (0, j))
    z, n0, n1 = pl.pallas_call(
        _conv_step_kernel,
        grid=(ncb,),
        in_specs=[row_spec(0), row_spec(1), row_spec(2), b_spec(0), b_spec(1),
                  pl.BlockSpec((3, tc), lambda j: (0, j))],
        out_specs=[o_spec, o_spec, o_spec],
        out_shape=[jax.ShapeDtypeStruct((n_seq, d), BF16),
                   jax.ShapeDtypeStruct((n_seq, d), F32),
                   jax.ShapeDtypeStruct((n_seq, d), F32)],
        compiler_params=_params(("arbitrary",)),
        name="conv_step",
    )(proj, proj, proj, buf2, buf2, conv_w)
    return z, jnp.stack([n0, n1], axis=1)


def _hgrn_lower_bound(logit_ref, layer):
    lg = logit_ref[...]
    rows = [lg[i:i + 1, :] for i in range(lg.shape[0])]
    mx = functools.reduce(jnp.maximum, rows)
    es = [jnp.exp(r - mx) for r in rows]
    tot = functools.reduce(lambda a, b: a + b, es)
    part = functools.reduce(lambda a, b: a + b, es[1:layer + 1])
    return part / tot


def _hgrn_gate_out(o, gn, g):
    o = o * lax.rsqrt(jnp.mean(o * o, axis=-1, keepdims=True) + RMS_EPS)
    return o * gn * jax.nn.sigmoid(g)


def _hgrn_prompt_kernel(q_ref, f_ref, v_ref, g_ref, lgt_ref, gn_ref, s0_ref, z_ref, so_ref, st_ref,
                        *, layer, hb, tb, L):
    tblk = pl.program_id(2)
    lb = _hgrn_lower_bound(lgt_ref, layer)
    gn = gn_ref[...]

    @pl.when(tblk == 0)
    def _():
        for h in range(hb):
            st_ref[h] = s0_ref[0, h].T

    ri = lax.broadcasted_iota(jnp.int32, (L, L), 0)
    ci = lax.broadcasted_iota(jnp.int32, (L, L), 1)
    causal = ci <= ri
    tri = causal.astype(F32)
    mid = L // 2

    def chunk(c, carry):
        r0 = pl.multiple_of(c * L, L)
        fpre = f_ref[pl.ds(r0, L), :]
        f = lb + (1.0 - lb) * jax.nn.sigmoid(fpre)
        logf = jnp.log(f)
        kk = 1.0 - f
        b = jnp.dot(tri, logf, preferred_element_type=F32, precision=lax.Precision.HIGHEST)
        q = jax.nn.silu(q_ref[pl.ds(r0, L), :])
        v = v_ref[pl.ds(r0, L), :]
        g = g_ref[pl.ds(r0, L), :]
        b_end = b[L - 1:L, :]
        b_mid = b[mid - 1:mid, :]
        qt = (q * jnp.exp(b - b_mid)).astype(BF16)
        kt = (kk * jnp.exp(b_mid - b)).astype(BF16)
        qd = (q * jnp.exp(b)).astype(BF16)
        kd = (kk * jnp.exp(b_end - b)).astype(BF16)
        vb = v.astype(BF16)
        dec = jnp.exp(b_end)
        outs = []
        for h in range(hb):
            sl = slice(h * HG_DK, (h + 1) * HG_DK)
            att = lax.dot_general(qt[:, sl], kt[:, sl], (((1,), (1,)), ((), ())),
                                  preferred_element_type=F32)
            att = jnp.where(causal, att, 0.0).astype(BF16)
            s_t = st_ref[h]
            o = jnp.dot(att, vb[:, sl], preferred_element_type=F32)
            o = o + lax.dot_general(qd[:, sl], s_t.astype(BF16), (((1,), (1,)), ((), ())),
                                    preferred_element_type=F32)
            upd = lax.dot_general(vb[:, sl], kd[:, sl], (((0,), (0,)), ((), ())),
                                  preferred_element_type=F32)
            st_ref[h] = s_t * dec[:, sl] + upd
            outs.append(_hgrn_gate_out(o, gn[:, sl], g[:, sl]))
        z_ref[pl.ds(r0, L), :] = jnp.concatenate(outs, axis=1).astype(z_ref.dtype)
        return carry

    lax.fori_loop(0, tb // L, chunk, 0)

    @pl.when(tblk == pl.num_programs(2) - 1)
    def _():
        for h in range(hb):
            so_ref[0, h] = st_ref[h].T


def hgrn_prompt(proj, n_seq, t_len, s0, lb_logits, gnorm, layer):
    d = proj.shape[1] // 4
    heads = d // HG_DK
    hb = _pick(heads, (4, 2, 1))
    nhb = heads // hb
    L = HG_CHUNK if t_len % HG_CHUNK == 0 else t_len
    tb = _pick(t_len, (512, 256, 128, 64))
    if tb % L:
        tb = t_len
    ntb = t_len // tb
    w = hb * HG_DK
    sec = lambda k: pl.BlockSpec((tb, w), lambda n, j, t: (n * ntb + t, k * nhb + j))
    kern = functools.partial(_hgrn_prompt_kernel, layer=layer, hb=hb, tb=tb, L=L)
    return pl.pallas_call(
        kern,
        grid=(n_seq, nhb, ntb),
        in_specs=[sec(0), sec(1), sec(2), sec(3),
                  pl.BlockSpec((lb_logits.shape[0], w), lambda n, j, t: (0, j)),
                  pl.BlockSpec((1, w), lambda n, j, t: (0, j)),
                  pl.BlockSpec((1, hb, HG_DK, HG_DK), lambda n, j, t: (n, j, 0, 0))],
        out_specs=[pl.BlockSpec((tb, w), lambda n, j, t: (n * ntb + t, j)),
                   pl.BlockSpec((1, hb, HG_DK, HG_DK), lambda n, j, t: (n, j, 0, 0))],
        out_shape=[jax.ShapeDtypeStruct((n_seq * t_len, d), BF16),
                   jax.ShapeDtypeStruct(s0.shape, F32)],
        scratch_shapes=[pltpu.VMEM((hb, HG_DK, HG_DK), F32)],
        compiler_params=_params(("arbitrary", "arbitrary", "arbitrary")),
        name="hgrn_prompt",
    )(proj, proj, proj, proj, lb_logits, gnorm.reshape(1, d), s0)


def _hgrn_step_kernel(q_ref, f_ref, v_ref, g_ref, lgt_ref, gn_ref, s_ref, z_ref, so_ref, *, layer, nb):
    lb = _hgrn_lower_bound(lgt_ref, layer)
    f = lb + (1.0 - lb) * jax.nn.sigmoid(f_ref[...])
    q = jax.nn.silu(q_ref[...])
    v = v_ref[...]
    f_t = f.T
    k_t = 1.0 - f_t
    q_t = q.T
    outs = []
    for i in range(nb):
        s_new = f_t[:, i:i + 1] * s_ref[i, 0] + k_t[:, i:i + 1] * v[i:i + 1, :]
        so_ref[i, 0] = s_new
        outs.append(jnp.sum(q_t[:, i:i + 1] * s_new, axis=0, keepdims=True))
    o = jnp.concatenate(outs, axis=0)
    z_ref[...] = _hgrn_gate_out(o, gn_ref[...], g_ref[...]).astype(z_ref.dtype)


def hgrn_step(proj, row_start, n_seq, s0, lb_logits, gnorm, layer):
    d = proj.shape[1] // 4
    heads = d // HG_DK
    nb = _pick(n_seq, (32, 16, 8))
    nnb = n_seq // nb
    rb = row_start // nb
    sec = lambda k: pl.BlockSpec((nb, HG_DK), lambda h, i: (rb + i, k * heads + h))
    s_spec = pl.BlockSpec((nb, 1, HG_DK, HG_DK), lambda h, i: (i, h, 0, 0))
    kern = functools.partial(_hgrn_step_kernel, layer=layer, nb=nb)
    return pl.pallas_call(
        kern,
        grid=(heads, nnb),
        in_specs=[sec(0), sec(1), sec(2), sec(3),
                  pl.BlockSpec((lb_logits.shape[0], HG_DK), lambda h, i: (0, h)),
                  pl.BlockSpec((1, HG_DK), lambda h, i: (0, h)),
                  s_spec],
        out_specs=[pl.BlockSpec((nb, HG_DK), lambda h, i: (i, h)), s_spec],
        out_shape=[jax.ShapeDtypeStruct((n_seq, d), BF16),
                   jax.ShapeDtypeStruct(s0.shape, F32)],
        compiler_params=_params(("arbitrary", "arbitrary")),
        name="hgrn_step",
    )(proj, proj, proj, proj, lb_logits, gnorm.reshape(1, d), s0)


def _pad_cols(w, n_pad):
    return jnp.pad(w, ((0, 0), (0, n_pad - w.shape[1])))


def kernel(x_prompt, x_sample, state_s5_re, state_s5_im, state_conv, state_hgrn,
           norm_mix, norm_ffn, norm_final,
           s5_a_re, s5_a_im, s5_log_dt, s5_b_re, s5_b_im, s5_c_re, s5_c_im, s5_d, s5_w_glu,
           conv_w_in, conv_w, conv_w_out,
           hgrn_w_in, hgrn_lb_logits, hgrn_gnorm, hgrn_w_out,
           ffn_w_in, ffn_w_out):
    nb, t_len, d = x_prompt.shape
    ns = x_sample.shape[0]
    assert x_sample.shape[1] == 1
    mp = nb * t_len
    depth = norm_mix.shape[0]
    gp = s5_a_re.shape[1] * s5_a_re.shape[2]
    dff = ffn_w_out.shape[1]
    dff_pad = -(-dff // 512) * 512

    h = jnp.concatenate([x_prompt.reshape(mp, d), x_sample.reshape(ns, d)], axis=0)
    z_state = jnp.zeros((nb, gp), F32)

    p_re, p_im, s_re, s_im, p_conv, s_conv, p_hg, s_hg = [], [], [], [], [], [], [], []
    for layer in range(depth):
        kind, j = layer % N_MIXERS, layer // N_MIXERS
        if kind == 0:
            u = rmsnorm_rows(h, norm_mix[layer], 0, mp + ns)
            prm = (s5_a_re[j], s5_a_im[j], s5_log_dt[j], s5_b_re[j], s5_b_im[j],
                   s5_c_re[j], s5_c_im[j], s5_d[j])
            zp, pre_, pim_ = s5_core(u, 0, nb, t_len, z_state, z_state, *prm)
            zs, sre_, sim_ = s5_core(u, mp, ns, 1, state_s5_re[j].reshape(ns, gp),
                                     state_s5_im[j].reshape(ns, gp), *prm)
            p_re.append(pre_)
            p_im.append(pim_)
            s_re.append(sre_)
            s_im.append(sim_)
            z = jnp.concatenate([zp, zs], axis=0)
            wg = s5_w_glu[j].astype(BF16)
            h = dense(z, [wg[:, :d], wg[:, d:]], epi=_epi_glu_res, out_dtype=F32, res=h, name="s5_glu")
        elif kind == 1:
            proj = dense(h, [conv_w_in[j].astype(BF16)], epi=_epi_plain, out_dtype=F32,
                         norm_w=norm_mix[layer], name="conv_in")
            zp, pb = conv_prompt(proj, nb, t_len, jnp.zeros((nb, 2, d), F32), conv_w[j])
            zs, sb = conv_step(proj, mp, ns, state_conv[j], conv_w[j])
            p_conv.append(pb)
            s_conv.append(sb)
            z = jnp.concatenate([zp, zs], axis=0)
            h = dense(z, [conv_w_out[j].astype(BF16)], epi=_epi_res, out_dtype=F32, res=h, name="conv_out")
        else:
            proj = dense(h, [hgrn_w_in[j].astype(BF16)], epi=_epi_plain, out_dtype=F32,
                         norm_w=norm_mix[layer], name="hgrn_in")
            heads = d // HG_DK
            zp, ps = hgrn_prompt(proj, nb, t_len, jnp.zeros((nb, heads, HG_DK, HG_DK), F32),
                                 hgrn_lb_logits, hgrn_gnorm[j], layer)
            zs, ss = hgrn_step(proj, mp, ns, state_hgrn[j], hgrn_lb_logits, hgrn_gnorm[j], layer)
            p_hg.append(ps)
            s_hg.append(ss)
            z = jnp.concatenate([zp, zs], axis=0)
            h = dense(z, [hgrn_w_out[j].astype(BF16)], epi=_epi_res, out_dtype=F32, res=h, name="hgrn_out")
        w_in = ffn_w_in[layer]
        w_gate = _pad_cols(w_in[:, :dff], dff_pad).astype(BF16)
        w_up = _pad_cols(w_in[:, dff:], dff_pad).astype(BF16)
        w_out = jnp.pad(ffn_w_out[layer], ((0, dff_pad - dff), (0, 0))).astype(BF16)
        hf = dense(h, [w_gate, w_up], epi=_epi_swiglu, out_dtype=BF16, norm_w=norm_ffn[layer], name="ffn_in")
        h = dense(hf, [w_out], epi=_epi_res, out_dtype=F32, res=h, tn=256, name="ffn_out")

    y_prompt = rmsnorm_rows(h, norm_final, 0, mp).reshape(nb, t_len, d)
    y_sample = rmsnorm_rows(h, norm_final, mp, ns).reshape(ns, 1, d)
    g, p = s5_a_re.shape[1], s5_a_re.shape[2]
    heads = d // HG_DK
    return (y_prompt, y_sample,
            jnp.stack(p_re).reshape(-1, nb, g, p), jnp.stack(p_im).reshape(-1, nb, g, p),
            jnp.stack(p_conv), jnp.stack(p_hg),
            jnp.stack(s_re).reshape(-1, ns, g, p), jnp.stack(s_im).reshape(-1, ns, g, p),
            jnp.stack(s_conv), jnp.stack(s_hg))
```

```python
import functools
import math

import jax
import jax.numpy as jnp
from jax import lax
from jax.experimental import pallas as pl
from jax.experimental.pallas import tpu as pltpu

F32 = jnp.float32
BF16 = jnp.bfloat16

RMS_EPS = 1e-6
N_MIXERS = 3
S5_GROUP = 16
S5_STATE = 64
S5_CHUNK = 8
HG_DK = 128
HG_CHUNK = 64
LANES = 128
S5_GPL = LANES // S5_GROUP
S5_HALF = S5_GPL * S5_STATE
VMEM_LIMIT = 56 * 1024 * 1024


def _params(sem):
    return pltpu.CompilerParams(dimension_semantics=sem, vmem_limit_bytes=VMEM_LIMIT)


def _pick(dim, candidates):
    for c in candidates:
        if dim % c == 0:
            return c
    return dim


def _rms(x, w):
    return x * lax.rsqrt(jnp.mean(x * x, axis=-1, keepdims=True) + RMS_EPS) * w


def _rmsnorm_kernel(x_ref, w_ref, o_ref, *, lane_groups):
    y = _rms(x_ref[...], w_ref[...]).astype(o_ref.dtype)
    if lane_groups:
        for g in range(o_ref.shape[0]):
            o_ref[g] = y[:, g * LANES:(g + 1) * LANES]
    else:
        o_ref[...] = y


def rmsnorm_rows(x, w, row_start, n_rows, out_dtype, lane_groups=False):
    d = x.shape[1]
    tm = _pick(math.gcd(row_start, n_rows) if row_start else n_rows, (128, 64, 32, 16, 8))
    off = row_start // tm
    if lane_groups:
        out_spec = pl.BlockSpec((d // LANES, tm, LANES), lambda i: (0, i, 0))
        out_shape = jax.ShapeDtypeStruct((d // LANES, n_rows, LANES), out_dtype)
    else:
        out_spec = pl.BlockSpec((tm, d), lambda i: (i, 0))
        out_shape = jax.ShapeDtypeStruct((n_rows, d), out_dtype)
    return pl.pallas_call(
        functools.partial(_rmsnorm_kernel, lane_groups=lane_groups),
        grid=(n_rows // tm,),
        in_specs=[pl.BlockSpec((tm, d), lambda i: (i + off, 0)),
                  pl.BlockSpec((1, d), lambda i: (0, 0))],
        out_specs=out_spec,
        out_shape=out_shape,
        compiler_params=_params(("arbitrary",)),
        name="rmsnorm",
    )(x, w.reshape(1, d))


def _epi_plain(accs, res):
    return accs[0]


def _epi_res(accs, res):
    return res + accs[0]


def _epi_glu_res(accs, res):
    return res + accs[0] * jax.nn.sigmoid(accs[1])


def _epi_swiglu(accs, res):
    return jax.nn.silu(accs[0]) * accs[1]


def _dense_kernel(*refs, n_w, has_res, epi, x_lg):
    x_ref = refs[0]
    w_refs = refs[1:1 + n_w]
    res_ref = refs[1 + n_w] if has_res else None
    o_ref = refs[1 + n_w + has_res]
    wb_refs = refs[2 + n_w + has_res:]

    @pl.when(pl.program_id(1) == 0)
    def _cast():
        for w_ref, wb_ref in zip(w_refs, wb_refs):
            kk = w_ref.shape[0]
            rc = kk // 8
            for r in range(0, kk, rc):
                wb_ref[r:r + rc, :] = w_ref[r:r + rc, :].astype(BF16)

    if x_lg:
        xb = jnp.concatenate([x_ref[g] for g in range(x_ref.shape[0])], axis=1)
    else:
        xb = x_ref[...]
    accs = [jnp.dot(xb, wb[...], preferred_element_type=F32) for wb in wb_refs]
    res = res_ref[...] if has_res else None
    o_ref[...] = epi(accs, res).astype(o_ref.dtype)


def dense(x, ws, n_out, *, epi, out_dtype, tm, tn, res=None, kk=None, x_kblock=0, x_lg=False,
          name="dense"):
    if x_lg:
        m = x.shape[1]
        kk = x.shape[0] * LANES
        x_spec = pl.BlockSpec((x.shape[0], tm, LANES), lambda j, i: (0, i, 0))
    else:
        m = x.shape[0]
        kk = kk or x.shape[1]
        x_spec = pl.BlockSpec((tm, kk), lambda j, i: (i, x_kblock))
    assert m % tm == 0 and n_out % tn == 0
    in_specs = [x_spec]
    args = [x]
    for w, rb, cb in ws:
        in_specs.append(pl.BlockSpec((kk, tn), lambda j, i, rb=rb, cb=cb: (rb, cb + j)))
        args.append(w)
    if res is not None:
        in_specs.append(pl.BlockSpec((tm, tn), lambda j, i: (i, j)))
        args.append(res)
    kern = functools.partial(_dense_kernel, n_w=len(ws), has_res=res is not None, epi=epi, x_lg=x_lg)
    return pl.pallas_call(
        kern,
        grid=(n_out // tn, m // tm),
        in_specs=in_specs,
        out_specs=pl.BlockSpec((tm, tn), lambda j, i: (i, j)),
        out_shape=jax.ShapeDtypeStruct((m, n_out), out_dtype),
        scratch_shapes=[pltpu.VMEM((kk, tn), BF16) for _ in ws],
        compiler_params=_params(("arbitrary", "arbitrary")),
        name=name,
    )(*args)


def _s5_kernel(*refs, L, nc, n_seq, aliased):
    (u_ref, sre_ref, sim_ref, ar_ref, ai_ref, ldt_ref, br_ref, bi_ref, cr_ref, ci_ref, d_ref) = refs[:11]
    z_ref, ore_ref, oim_ref, tmat, qmat, pmat, binp, xst = refs[11 + aliased:]
    H = S5_HALF

    ar = ar_ref[...]
    ai = ai_ref[...]
    dt = jnp.exp(ldt_ref[...])
    mag = jnp.exp(ar * dt)
    lr = mag * jnp.cos(ai * dt)
    li = mag * jnp.sin(ai * dt)
    den = ar * ar + ai * ai
    cor = ((lr - 1.0) * ar + li * ai) / den
    coi = (li * ar - (lr - 1.0) * ai) / den

    pows = [(jnp.ones_like(lr), jnp.zeros_like(lr))]
    for _ in range(L):
        pr, pi = pows[-1]
        pows.append((pr * lr - pi * li, pr * li + pi * lr))

    row_g = lax.broadcasted_iota(jnp.int32, (LANES, H), 0) // S5_GROUP
    lane_g = lax.broadcasted_iota(jnp.int32, (LANES, H), 1) // S5_STATE
    diag = row_g == lane_g
    br = jnp.where(diag, br_ref[...], 0.0)
    bi = jnp.where(diag, bi_ref[...], 0.0)
    cr = jnp.where(diag, cr_ref[...], 0.0)
    ci = jnp.where(diag, ci_ref[...], 0.0)
    a_r = br * cor - bi * coi
    a_i = br * coi + bi * cor
    a_cat = jnp.concatenate([a_r, a_i], axis=1).astype(BF16)
    zero_blk = jnp.zeros((LANES, LANES), BF16)
    for k in range(L + 1):
        pr, pi = pows[k]
        w_k = jnp.concatenate([cr * pr - ci * pi, -(cr * pi + ci * pr)], axis=1).astype(BF16)
        if k >= 1:
            pmat[(k - 1) * LANES:k * LANES, :] = w_k
        if k < L:
            d_k = lax.dot_general(a_cat, w_k, (((1,), (1,)), ((), ())),
                                  preferred_element_type=F32).astype(BF16)
            for s in range(L - k):
                t = s + k
                tmat[s * LANES:(s + 1) * LANES, t * LANES:(t + 1) * LANES] = d_k
            s = L - 1 - k
            qmat[s * LANES:(s + 1) * LANES, :] = jnp.concatenate(
                [a_r * pr - a_i * pi, a_r * pi + a_i * pr], axis=1).astype(BF16)
    for s in range(L):
        for t in range(s):
            tmat[s * LANES:(s + 1) * LANES, t * LANES:(t + 1) * LANES] = zero_blk

    u = u_ref[...]
    xcat = u.astype(BF16)
    y = jnp.dot(xcat, tmat[...], preferred_element_type=F32)
    b_in = jnp.dot(xcat, qmat[...], preferred_element_type=F32)
    lLr, lLi = pows[L]

    if nc == 1:
        xr = sre_ref[...]
        xi = sim_ref[...]
        xstart = jnp.concatenate([xr, xi], axis=1)
        ore_ref[...] = xr * lLr - xi * lLi + b_in[:, :H]
        oim_ref[...] = xr * lLi + xi * lLr + b_in[:, H:]
    else:
        hp = H // LANES
        lanes = lambda a, k: a[:, k * LANES:(k + 1) * LANES]
        for k in range(2 * hp):
            binp[k] = lanes(b_in, k)
        lam_r = [lanes(lLr, k) for k in range(hp)]
        lam_i = [lanes(lLi, k) for k in range(hp)]

        def step(c, carry):
            rows = pl.ds(c, n_seq, stride=nc)
            new = []
            for k in range(hp):
                xr, xi = carry[k], carry[hp + k]
                xst[k, rows, :] = xr
                xst[hp + k, rows, :] = xi
                new.append((xr * lam_r[k] - xi * lam_i[k] + binp[k, rows, :],
                            xr * lam_i[k] + xi * lam_r[k] + binp[hp + k, rows, :]))
            return tuple(r for r, _ in new) + tuple(i for _, i in new)

        x0r = sre_ref[...]
        x0i = sim_ref[...]
        init = tuple(lanes(x0r, k) for k in range(hp)) + tuple(lanes(x0i, k) for k in range(hp))
        fin = lax.fori_loop(0, nc, step, init, unroll=4)
        ore_ref[...] = jnp.concatenate(fin[:hp], axis=1)
        oim_ref[...] = jnp.concatenate(fin[hp:], axis=1)
        xstart = jnp.concatenate([xst[k] for k in range(2 * hp)], axis=1)
    y = y + lax.dot_general(xstart.astype(BF16), pmat[...], (((1,), (1,)), ((), ())),
                            preferred_element_type=F32)
    dsk = jnp.concatenate([d_ref[...]] * L, axis=1) if L > 1 else d_ref[...]
    z_ref[...] = jax.nn.gelu(y + dsk * u).astype(z_ref.dtype)


def _s5_lane_rows(a):
    g, p = a.shape
    return a.reshape(g // S5_GPL, 1, S5_GPL * p)


def _s5_tiles(a):
    g, c, p = a.shape
    return jnp.tile(a.reshape(g // S5_GPL, S5_GPL * c, p), (1, 1, S5_GPL))


def s5_core(u_lg, z_prev, row_start, n_seq, t_len, st_re, st_im,
            a_re, a_im, log_dt, b_re, b_im, c_re, c_im, d_skip):
    nlg, m, _ = u_lg.shape
    g, p = a_re.shape
    H = S5_HALF
    L = S5_CHUNK if t_len % S5_CHUNK == 0 else 1
    assert L > 1 or t_len == 1
    nc = t_len // L
    rows = n_seq * nc
    assert row_start % (rows * L) == 0 and m % L == 0
    rb = row_start // (rows * L)
    lane_args = [_s5_lane_rows(a_re), _s5_lane_rows(a_im),
                 _s5_lane_rows(jnp.broadcast_to(log_dt[:, None], (g, p)))]
    tile_args = [_s5_tiles(jnp.swapaxes(b_re, 1, 2)), _s5_tiles(jnp.swapaxes(b_im, 1, 2)),
                 _s5_tiles(c_re), _s5_tiles(c_im)]
    d_arg = d_skip.reshape(nlg, 1, LANES)
    lane_spec = pl.BlockSpec((None, 1, H), lambda j: (j, 0, 0))
    tile_spec = pl.BlockSpec((None, LANES, H), lambda j: (j, 0, 0))
    d_spec = pl.BlockSpec((None, 1, LANES), lambda j: (j, 0, 0))
    act_spec = pl.BlockSpec((None, rows, L * LANES), lambda j: (j, rb, 0))
    st_spec = pl.BlockSpec((n_seq, H), lambda j: (0, j))
    st_shape = jax.ShapeDtypeStruct((n_seq, g * p), F32)
    in_specs = [act_spec, st_spec, st_spec] + [lane_spec] * 3 + [tile_spec] * 4 + [d_spec]
    args = [u_lg.reshape(nlg, m // L, L * LANES), st_re, st_im, *lane_args, *tile_args, d_arg]
    aliases = {}
    if z_prev is not None:
        in_specs.append(pl.BlockSpec(memory_space=pl.ANY))
        args.append(z_prev.reshape(nlg, m // L, L * LANES))
        aliases = {len(args) - 1: 0}
    kern = functools.partial(_s5_kernel, L=L, nc=nc, n_seq=n_seq, aliased=z_prev is not None)
    z, o_re, o_im = pl.pallas_call(
        kern,
        grid=(nlg,),
        in_specs=in_specs,
        out_specs=[act_spec, st_spec, st_spec],
        out_shape=[jax.ShapeDtypeStruct((nlg, m // L, L * LANES), BF16), st_shape, st_shape],
        scratch_shapes=[pltpu.VMEM((L * LANES, L * LANES), BF16),
                        pltpu.VMEM((L * LANES, 2 * H), BF16),
                        pltpu.VMEM((L * LANES, 2 * H), BF16),
                        pltpu.VMEM((2 * H // LANES, rows, LANES), F32),
                        pltpu.VMEM((2 * H // LANES, rows, LANES), F32)],
        input_output_aliases=aliases,
        compiler_params=_params(("arbitrary",)),
        name="s5_core",
    )(*args)
    return z.reshape(nlg, m, LANES), o_re, o_im


def _conv_prompt_kernel(gb_ref, gc_ref, v_ref, buf_ref, cw_ref, z_ref, nb_ref):
    pre = gc_ref[...] * v_ref[...]
    t = pre.shape[0]
    buf = buf_ref[...]
    row = lax.broadcasted_iota(jnp.int32, pre.shape, 0)
    sh1 = jnp.where(row == 0, buf[1:2, :], pltpu.roll(pre, 1, 0))
    sh2 = pltpu.roll(pre, 2, 0)
    sh2 = jnp.where(row == 0, buf[0:1, :], jnp.where(row == 1, buf[1:2, :], sh2))
    cw = cw_ref[...]
    conv = cw[0:1, :] * sh2 + cw[1:2, :] * sh1 + cw[2:3, :] * pre
    z_ref[...] = (gb_ref[...] * conv).astype(z_ref.dtype)
    nb_ref[...] = pre[t - 2:t, :]


def conv_prompt(proj, n_seq, t_len, buf, conv_w):
    d = proj.shape[1] // 3
    tc = _pick(d, (256, 128))
    ncb = d // tc
    return pl.pallas_call(
        _conv_prompt_kernel,
        grid=(n_seq, ncb),
        in_specs=[pl.BlockSpec((t_len, tc), lambda n, j: (n, j)),
                  pl.BlockSpec((t_len, tc), lambda n, j: (n, ncb + j)),
                  pl.BlockSpec((t_len, tc), lambda n, j: (n, 2 * ncb + j)),
                  pl.BlockSpec((None, 2, tc), lambda n, j: (n, 0, j)),
                  pl.BlockSpec((3, tc), lambda n, j: (0, j))],
        out_specs=[pl.BlockSpec((t_len, tc), lambda n, j: (n, j)),
                   pl.BlockSpec((None, 2, tc), lambda n, j: (n, 0, j))],
        out_shape=[jax.ShapeDtypeStruct((n_seq * t_len, d), BF16),
                   jax.ShapeDtypeStruct((n_seq, 2, d), F32)],
        compiler_params=_params(("arbitrary", "arbitrary")),
        name="conv_prompt",
    )(proj, proj, proj, buf, conv_w)


def _conv_step_kernel(gb_ref, gc_ref, v_ref, b0_ref, b1_ref, cw_ref, z_ref, n0_ref, n1_ref):
    pre = gc_ref[...] * v_ref[...]
    b0 = b0_ref[...]
    b1 = b1_ref[...]
    cw = cw_ref[...]
    conv = cw[0:1, :] * b0 + cw[1:2, :] * b1 + cw[2:3, :] * pre
    z_ref[...] = (gb_ref[...] * conv).astype(z_ref.dtype)
    n0_ref[...] = b1
    n1_ref[...] = pre


def conv_step(proj, row_start, n_seq, buf, conv_w):
    d = proj.shape[1] // 3
    tc = _pick(d, (512, 256, 128))
    ncb = d // tc
    rb = row_start // n_seq
    buf2 = buf.reshape(n_seq, 2 * d)
    row_spec = lambda off: pl.BlockSpec((n_seq, tc), lambda j: (rb, off * ncb + j))
    b_spec = lambda off: pl.BlockSpec((n_seq, tc), lambda j: (0, off * ncb + j))
    o_spec = pl.BlockSpec((n_seq, tc), lambda j: (0, j))
    z, n0, n1 = pl.pallas_call(
        _conv_step_kernel,
        grid=(ncb,),
        in_specs=[row_spec(0), row_spec(1), row_spec(2), b_spec(0), b_spec(1),
                  pl.BlockSpec((3, tc), lambda j: (0, j))],
        out_specs=[o_spec, o_spec, o_spec],
        out_shape=[jax.ShapeDtypeStruct((n_seq, d), BF16),
                   jax.ShapeDtypeStruct((n_seq, d), F32),
                   jax.ShapeDtypeStruct((n_seq, d), F32)],
        compiler_params=_params(("arbitrary",)),
        name="conv_step",
    )(proj, proj, proj, buf2, buf2, conv_w)
    return z, jnp.stack([n0, n1], axis=1)


def _hgrn_lower_bound(logit_ref, layer):
    lg = logit_ref[...]
    rows = [lg[i:i + 1, :] for i in range(lg.shape[0])]
    mx = functools.reduce(jnp.maximum, rows)
    es = [jnp.exp(r - mx) for r in rows]
    tot = functools.reduce(lambda a, b: a + b, es)
    part = functools.reduce(lambda a, b: a + b, es[1:layer + 1])
    return part / tot


def _hgrn_gate_out(o, gn, g):
    o = o * lax.rsqrt(jnp.mean(o * o, axis=-1, keepdims=True) + RMS_EPS)
    return o * gn * jax.nn.sigmoid(g)


def _hgrn_prompt_kernel(q_ref, f_ref, v_ref, g_ref, lgt_ref, gn_ref, s0_ref, z_ref, so_ref, st_ref,
                        *, layer, hb, tb, L):
    tblk = pl.program_id(2)
    lb = _hgrn_lower_bound(lgt_ref, layer)
    gn = gn_ref[...]

    @pl.when(tblk == 0)
    def _():
        for h in range(hb):
            st_ref[h] = s0_ref[0, h].T

    ri = lax.broadcasted_iota(jnp.int32, (L, L), 0)
    ci = lax.broadcasted_iota(jnp.int32, (L, L), 1)
    causal = ci <= ri
    tri = causal.astype(F32)
    mid = L // 2

    def chunk(c, carry):
        r0 = pl.multiple_of(c * L, L)
        fpre = f_ref[pl.ds(r0, L), :]
        f = lb + (1.0 - lb) * jax.nn.sigmoid(fpre)
        logf = jnp.log(f)
        kk = 1.0 - f
        b = jnp.dot(tri, logf, preferred_element_type=F32, precision=lax.Precision.HIGHEST)
        q = jax.nn.silu(q_ref[pl.ds(r0, L), :])
        v = v_ref[pl.ds(r0, L), :]
        g = g_ref[pl.ds(r0, L), :]
        b_end = b[L - 1:L, :]
        b_mid = b[mid - 1:mid, :]
        qt = (q * jnp.exp(b - b_mid)).astype(BF16)
        kt = (kk * jnp.exp(b_mid - b)).astype(BF16)
        qd = (q * jnp.exp(b)).astype(BF16)
        kd = (kk * jnp.exp(b_end - b)).astype(BF16)
        vb = v.astype(BF16)
        dec = jnp.exp(b_end)
        outs = []
        for h in range(hb):
            sl = slice(h * HG_DK, (h + 1) * HG_DK)
            att = lax.dot_general(qt[:, sl], kt[:, sl], (((1,), (1,)), ((), ())),
                                  preferred_element_type=F32)
            att = jnp.where(causal, att, 0.0).astype(BF16)
            s_t = st_ref[h]
            o = jnp.dot(att, vb[:, sl], preferred_element_type=F32)
            o = o + lax.dot_general(qd[:, sl], s_t.astype(BF16), (((1,), (1,)), ((), ())),
                                    preferred_element_type=F32)
            upd = lax.dot_general(vb[:, sl], kd[:, sl], (((0,), (0,)), ((), ())),
                                  preferred_element_type=F32)
            st_ref[h] = s_t * dec[:, sl] + upd
            outs.append(_hgrn_gate_out(o, gn[:, sl], g[:, sl]))
        z_ref[pl.ds(r0, L), :] = jnp.concatenate(outs, axis=1).astype(z_ref.dtype)
        return carry

    lax.fori_loop(0, tb // L, chunk, 0, unroll=2)

    @pl.when(tblk == pl.num_programs(2) - 1)
    def _():
        for h in range(hb):
            so_ref[0, h] = st_ref[h].T


def hgrn_prompt(proj, n_seq, t_len, s0, lb_logits, gnorm, layer):
    d = proj.shape[1] // 4
    heads = d // HG_DK
    hb = _pick(heads, (8, 4, 2, 1))
    nhb = heads // hb
    L = HG_CHUNK if t_len % HG_CHUNK == 0 else t_len
    tb = _pick(t_len, (512, 256, 128, 64))
    if tb % L:
        tb = t_len
    ntb = t_len // tb
    w = hb * HG_DK
    sec = lambda k: pl.BlockSpec((tb, w), lambda n, j, t: (n * ntb + t, k * nhb + j))
    kern = functools.partial(_hgrn_prompt_kernel, layer=layer, hb=hb, tb=tb, L=L)
    return pl.pallas_call(
        kern,
        grid=(n_seq, nhb, ntb),
        in_specs=[sec(0), sec(1), sec(2), sec(3),
                  pl.BlockSpec((lb_logits.shape[0], w), lambda n, j, t: (0, j)),
                  pl.BlockSpec((1, w), lambda n, j, t: (0, j)),
                  pl.BlockSpec((1, hb, HG_DK, HG_DK), lambda n, j, t: (n, j, 0, 0))],
        out_specs=[pl.BlockSpec((tb, w), lambda n, j, t: (n * ntb + t, j)),
                   pl.BlockSpec((1, hb, HG_DK, HG_DK), lambda n, j, t: (n, j, 0, 0))],
        out_shape=[jax.ShapeDtypeStruct((n_seq * t_len, d), BF16),
                   jax.ShapeDtypeStruct(s0.shape, F32)],
        scratch_shapes=[pltpu.VMEM((hb, HG_DK, HG_DK), F32)],
        compiler_params=_params(("arbitrary", "arbitrary", "arbitrary")),
        name="hgrn_prompt",
    )(proj, proj, proj, proj, lb_logits, gnorm.reshape(1, d), s0)


def _hgrn_step_kernel(q_ref, f_ref, v_ref, g_ref, lgt_ref, gn_ref, s_ref, z_ref, so_ref, *, layer, nb):
    lb = _hgrn_lower_bound(lgt_ref, layer)
    f = lb + (1.0 - lb) * jax.nn.sigmoid(f_ref[...])
    q = jax.nn.silu(q_ref[...])
    v = v_ref[...]
    f_t = f.T
    k_t = 1.0 - f_t
    q_t = q.T
    outs = []
    for i in range(nb):
        s_new = f_t[:, i:i + 1] * s_ref[i, 0] + k_t[:, i:i + 1] * v[i:i + 1, :]
        so_ref[i, 0] = s_new
        outs.append(jnp.sum(q_t[:, i:i + 1] * s_new, axis=0, keepdims=True))
    o = jnp.concatenate(outs, axis=0)
    z_ref[...] = _hgrn_gate_out(o, gn_ref[...], g_ref[...]).astype(z_ref.dtype)


def hgrn_step(proj, row_start, n_seq, s0, lb_logits, gnorm, layer):
    d = proj.shape[1] // 4
    heads = d // HG_DK
    nb = _pick(n_seq, (32, 16, 8))
    nnb = n_seq // nb
    rb = row_start // nb
    sec = lambda k: pl.BlockSpec((nb, HG_DK), lambda h, i: (rb + i, k * heads + h))
    s_spec = pl.BlockSpec((nb, 1, HG_DK, HG_DK), lambda h, i: (i, h, 0, 0))
    kern = functools.partial(_hgrn_step_kernel, layer=layer, nb=nb)
    return pl.pallas_call(
        kern,
        grid=(heads, nnb),
        in_specs=[sec(0), sec(1), sec(2), sec(3),
                  pl.BlockSpec((lb_logits.shape[0], HG_DK), lambda h, i: (0, h)),
                  pl.BlockSpec((1, HG_DK), lambda h, i: (0, h)),
                  s_spec],
        out_specs=[pl.BlockSpec((nb, HG_DK), lambda h, i: (i, h)), s_spec],
        out_shape=[jax.ShapeDtypeStruct((n_seq, d), BF16),
                   jax.ShapeDtypeStruct(s0.shape, F32)],
        compiler_params=_params(("arbitrary", "arbitrary")),
        name="hgrn_step",
    )(proj, proj, proj, proj, lb_logits, gnorm.reshape(1, d), s0)


def _row_tile(m, candidates):
    return _pick(m, candidates + (256, 128, 64, 32, 16))


def kernel(x_prompt, x_sample, state_s5_re, state_s5_im, state_conv, state_hgrn,
           norm_mix, norm_ffn, norm_final,
           s5_a_re, s5_a_im, s5_log_dt, s5_b_re, s5_b_im, s5_c_re, s5_c_im, s5_d, s5_w_glu,
           conv_w_in, conv_w, conv_w_out,
           hgrn_w_in, hgrn_lb_logits, hgrn_gnorm, hgrn_w_out,
           ffn_w_in, ffn_w_out):
    nb, t_len, d = x_prompt.shape
    ns = x_sample.shape[0]
    assert x_sample.shape[1] == 1
    mp = nb * t_len
    m = mp + ns
    depth = norm_mix.shape[0]
    g, p = s5_a_re.shape[1], s5_a_re.shape[2]
    gp = g * p
    dff = ffn_w_out.shape[1]
    heads = d // HG_DK
    tm_big = _row_tile(m, (1040,))
    tm_mid = _row_tile(m, (640,))
    tn_big = _pick(d, (512, 256, 128))
    tn_ff = _pick(dff, (256, 128))
    kk_ff = dff // 2 if (dff // 2) % LANES == 0 else dff

    h = jnp.concatenate([x_prompt.reshape(mp, d), x_sample.reshape(ns, d)], axis=0)
    z_state = jnp.zeros((nb, gp), F32)

    p_re, p_im, s_re, s_im, p_conv, s_conv, p_hg, s_hg = [], [], [], [], [], [], [], []
    for layer in range(depth):
        kind, j = layer % N_MIXERS, layer // N_MIXERS
        if kind == 0:
            u = rmsnorm_rows(h, norm_mix[layer], 0, m, F32, lane_groups=True)
            prm = (s5_a_re[j], s5_a_im[j], s5_log_dt[j], s5_b_re[j], s5_b_im[j],
                   s5_c_re[j], s5_c_im[j], s5_d[j])
            z, pre_, pim_ = s5_core(u, None, 0, nb, t_len, z_state, z_state, *prm)
            z, sre_, sim_ = s5_core(u, z, mp, ns, 1, state_s5_re[j].reshape(ns, gp),
                                    state_s5_im[j].reshape(ns, gp), *prm)
            p_re.append(pre_)
            p_im.append(pim_)
            s_re.append(sre_)
            s_im.append(sim_)
            tn = _pick(d, (256, 128))
            h = dense(z, [(s5_w_glu[j], 0, 0), (s5_w_glu[j], 0, d // tn)], d, epi=_epi_glu_res,
                      out_dtype=F32, res=h, tm=tm_big, tn=tn, x_lg=True, name="s5_glu")
        elif kind == 1:
            xn = rmsnorm_rows(h, norm_mix[layer], 0, m, BF16)
            proj = dense(xn, [(conv_w_in[j], 0, 0)], 3 * d, epi=_epi_plain, out_dtype=F32,
                         tm=tm_big, tn=tn_big, name="conv_in")
            zp, pb = conv_prompt(proj, nb, t_len, jnp.zeros((nb, 2, d), F32), conv_w[j])
            zs, sb = conv_step(proj, mp, ns, state_conv[j], conv_w[j])
            p_conv.append(pb)
            s_conv.append(sb)
            z = jnp.concatenate([zp, zs], axis=0)
            h = dense(z, [(conv_w_out[j], 0, 0)], d, epi=_epi_res, out_dtype=F32, res=h,
                      tm=tm_big, tn=tn_big, name="conv_out")
        else:
            xn = rmsnorm_rows(h, norm_mix[layer], 0, m, BF16)
            proj = dense(xn, [(hgrn_w_in[j], 0, 0)], 4 * d, epi=_epi_plain, out_dtype=F32,
                         tm=tm_big, tn=tn_big, name="hgrn_in")
            zp, ps = hgrn_prompt(proj, nb, t_len, jnp.zeros((nb, heads, HG_DK, HG_DK), F32),
                                 hgrn_lb_logits, hgrn_gnorm[j], layer)
            zs, ss = hgrn_step(proj, mp, ns, state_hgrn[j], hgrn_lb_logits, hgrn_gnorm[j], layer)
            p_hg.append(ps)
            s_hg.append(ss)
            z = jnp.concatenate([zp, zs], axis=0)
            h = dense(z, [(hgrn_w_out[j], 0, 0)], d, epi=_epi_res, out_dtype=F32, res=h,
                      tm=tm_big, tn=tn_big, name="hgrn_out")
        xn = rmsnorm_rows(h, norm_ffn[layer], 0, m, BF16)
        hf = dense(xn, [(ffn_w_in[layer], 0, 0), (ffn_w_in[layer], 0, dff // tn_ff)], dff,
                   epi=_epi_swiglu, out_dtype=BF16, tm=tm_big, tn=tn_ff, name="ffn_in")
        for kb in range(dff // kk_ff):
            h = dense(hf, [(ffn_w_out[layer], kb, 0)], d, epi=_epi_res, out_dtype=F32, res=h,
                      tm=tm_mid, tn=tn_big, kk=kk_ff, x_kblock=kb, name="ffn_out")

    y_prompt = rmsnorm_rows(h, norm_final, 0, mp, F32).reshape(nb, t_len, d)
    y_sample = rmsnorm_rows(h, norm_final, mp, ns, F32).reshape(ns, 1, d)
    return (y_prompt, y_sample,
            jnp.stack(p_re).reshape(-1, nb, g, p), jnp.stack(p_im).reshape(-1, nb, g, p),
            jnp.stack(p_conv), jnp.stack(p_hg),
            jnp.stack(s_re).reshape(-1, ns, g, p), jnp.stack(s_im).reshape(-1, ns, g, p),
            jnp.stack(s_conv), jnp.stack(s_hg))
```

```python
import functools
import math

import jax
import jax.numpy as jnp
from jax import lax
from jax.experimental import pallas as pl
from jax.experimental.pallas import tpu as pltpu

F32 = jnp.float32
BF16 = jnp.bfloat16

RMS_EPS = 1e-6
N_MIXERS = 3
S5_GROUP = 16
S5_STATE = 64
S5_CHUNK = 8
HG_DK = 128
HG_CHUNK = 64
LANES = 128
S5_GPL = LANES // S5_GROUP
S5_HALF = S5_GPL * S5_STATE
VMEM_LIMIT = 56 * 1024 * 1024


def _params(sem):
    return pltpu.CompilerParams(dimension_semantics=sem, vmem_limit_bytes=VMEM_LIMIT)


def _pick(dim, candidates):
    for c in candidates:
        if dim % c == 0:
            return c
    return dim


def _rms(x, w):
    return x * lax.rsqrt(jnp.mean(x * x, axis=-1, keepdims=True) + RMS_EPS) * w


def _rmsnorm_kernel(x_ref, w_ref, o_ref, *, lane_groups):
    tm = x_ref.shape[0]
    rc = _pick(tm, (64, 32, 16, 8))
    for r in range(0, tm, rc):
        y = _rms(x_ref[r:r + rc, :], w_ref[...]).astype(o_ref.dtype)
        if lane_groups:
            for g in range(o_ref.shape[0]):
                o_ref[g, r:r + rc, :] = y[:, g * LANES:(g + 1) * LANES]
        else:
            o_ref[r:r + rc, :] = y


def rmsnorm_rows(x, w, row_start, n_rows, out_dtype, lane_groups=False):
    d = x.shape[1]
    tm = _pick(math.gcd(row_start, n_rows) if row_start else n_rows, (320, 256, 128, 64, 32, 16, 8))
    off = row_start // tm
    if lane_groups:
        out_spec = pl.BlockSpec((d // LANES, tm, LANES), lambda i: (0, i, 0))
        out_shape = jax.ShapeDtypeStruct((d // LANES, n_rows, LANES), out_dtype)
    else:
        out_spec = pl.BlockSpec((tm, d), lambda i: (i, 0))
        out_shape = jax.ShapeDtypeStruct((n_rows, d), out_dtype)
    return pl.pallas_call(
        functools.partial(_rmsnorm_kernel, lane_groups=lane_groups),
        grid=(n_rows // tm,),
        in_specs=[pl.BlockSpec((tm, d), lambda i: (i + off, 0)),
                  pl.BlockSpec((1, d), lambda i: (0, 0))],
        out_specs=out_spec,
        out_shape=out_shape,
        compiler_params=_params(("arbitrary",)),
        name="rmsnorm",
    )(x, w.reshape(1, d))


def _epi_plain(accs, res):
    return accs[0]


def _epi_res(accs, res):
    return res + accs[0]


def _epi_glu_res(accs, res):
    return res + accs[0] * jax.nn.sigmoid(accs[1])


def _epi_swiglu(accs, res):
    return jax.nn.silu(accs[0]) * accs[1]


def _dense_kernel(*refs, n_w, has_res, epi, x_lg):
    x_ref = refs[0]
    w_refs = refs[1:1 + n_w]
    res_ref = refs[1 + n_w] if has_res else None
    o_ref = refs[1 + n_w + has_res]
    wb_refs = refs[2 + n_w + has_res:]

    @pl.when(pl.program_id(1) == 0)
    def _cast():
        for w_ref, wb_ref in zip(w_refs, wb_refs):
            kk = w_ref.shape[0]
            rc = kk // 8
            for r in range(0, kk, rc):
                wb_ref[r:r + rc, :] = w_ref[r:r + rc, :].astype(BF16)

    if x_lg:
        xb = jnp.concatenate([x_ref[g].astype(BF16) for g in range(x_ref.shape[0])], axis=1)
    else:
        xb = x_ref[...]
    accs = [jnp.dot(xb, wb[...], preferred_element_type=F32) for wb in wb_refs]
    res = res_ref[...] if has_res else None
    o_ref[...] = epi(accs, res).astype(o_ref.dtype)


def dense(x, ws, n_out, *, epi, out_dtype, tm, tn, res=None, kk=None, x_kblock=0, x_lg=False,
          name="dense"):
    if x_lg:
        m = x.shape[1]
        kk = x.shape[0] * LANES
        x_spec = pl.BlockSpec((x.shape[0], tm, LANES), lambda j, i: (0, i, 0))
    else:
        m = x.shape[0]
        kk = kk or x.shape[1]
        x_spec = pl.BlockSpec((tm, kk), lambda j, i: (i, x_kblock))
    assert m % tm == 0 and n_out % tn == 0
    in_specs = [x_spec]
    args = [x]
    for w, lyr, rb, cb in ws:
        in_specs.append(pl.BlockSpec((None, kk, tn), lambda j, i, lyr=lyr, rb=rb, cb=cb: (lyr, rb, cb + j)))
        args.append(w)
    if res is not None:
        in_specs.append(pl.BlockSpec((tm, tn), lambda j, i: (i, j)))
        args.append(res)
    kern = functools.partial(_dense_kernel, n_w=len(ws), has_res=res is not None, epi=epi, x_lg=x_lg)
    return pl.pallas_call(
        kern,
        grid=(n_out // tn, m // tm),
        in_specs=in_specs,
        out_specs=pl.BlockSpec((tm, tn), lambda j, i: (i, j)),
        out_shape=jax.ShapeDtypeStruct((m, n_out), out_dtype),
        scratch_shapes=[pltpu.VMEM((kk, tn), BF16) for _ in ws],
        compiler_params=_params(("arbitrary", "arbitrary")),
        name=name,
    )(*args)


def _s5_kernel(*refs, L, nc, n_seq, aliased):
    (u_ref, sre_ref, sim_ref, ar_ref, ai_ref, ldt_ref, br_ref, bi_ref, cr_ref, ci_ref, d_ref) = refs[:11]
    z_ref, ore_ref, oim_ref, tmat, qmat, pmat, binp, xst, xcat, xsb = refs[11 + aliased:]
    H = S5_HALF

    ar = ar_ref[...]
    ai = ai_ref[...]
    dt = jnp.exp(ldt_ref[...])
    mag = jnp.exp(ar * dt)
    lr = mag * jnp.cos(ai * dt)
    li = mag * jnp.sin(ai * dt)
    den = ar * ar + ai * ai
    cor = ((lr - 1.0) * ar + li * ai) / den
    coi = (li * ar - (lr - 1.0) * ai) / den

    pows = [(jnp.ones_like(lr), jnp.zeros_like(lr))]
    for _ in range(L):
        pr, pi = pows[-1]
        pows.append((pr * lr - pi * li, pr * li + pi * lr))

    row_g = lax.broadcasted_iota(jnp.int32, (LANES, H), 0) // S5_GROUP
    lane_g = lax.broadcasted_iota(jnp.int32, (LANES, H), 1) // S5_STATE
    diag = row_g == lane_g
    br = jnp.where(diag, br_ref[...], 0.0)
    bi = jnp.where(diag, bi_ref[...], 0.0)
    cr = jnp.where(diag, cr_ref[...], 0.0)
    ci = jnp.where(diag, ci_ref[...], 0.0)
    a_r = br * cor - bi * coi
    a_i = br * coi + bi * cor
    a_cat = jnp.concatenate([a_r, a_i], axis=1).astype(BF16)
    zero_blk = jnp.zeros((LANES, LANES), BF16)
    for k in range(L + 1):
        pr, pi = pows[k]
        w_k = jnp.concatenate([cr * pr - ci * pi, -(cr * pi + ci * pr)], axis=1).astype(BF16)
        if k >= 1:
            pmat[(k - 1) * LANES:k * LANES, :] = w_k
        if k < L:
            d_k = lax.dot_general(a_cat, w_k, (((1,), (1,)), ((), ())),
                                  preferred_element_type=F32).astype(BF16)
            for s in range(L - k):
                t = s + k
                tmat[s * LANES:(s + 1) * LANES, t * LANES:(t + 1) * LANES] = d_k
            s = L - 1 - k
            qmat[s * LANES:(s + 1) * LANES, :] = jnp.concatenate(
                [a_r * pr - a_i * pi, a_r * pi + a_i * pr], axis=1).astype(BF16)
    for s in range(L):
        for t in range(s):
            tmat[s * LANES:(s + 1) * LANES, t * LANES:(t + 1) * LANES] = zero_blk

    hp = H // LANES
    lanes = lambda a, k: a[:, k * LANES:(k + 1) * LANES]
    ct = 2 * LANES if L % 2 == 0 else LANES

    n_rows = n_seq * nc
    pos = lambda s: pl.ds(s, n_rows, stride=L) if L > 1 else slice(None)

    for s in range(L):
        xcat[:, s * LANES:(s + 1) * LANES] = u_ref[pos(s), :].astype(BF16)

    slots = binp.shape[1] // nc
    seq_rows = lambda n: pl.ds(n, nc, stride=slots)
    for k0 in range(0, 2 * hp, 2):
        b2 = jnp.dot(xcat[...], qmat[:, k0 * LANES:(k0 + 2) * LANES], preferred_element_type=F32)
        if nc == 1:
            binp[k0] = b2[:, :LANES]
            binp[k0 + 1] = b2[:, LANES:]
        else:
            for n in range(n_seq):
                binp[k0, seq_rows(n), :] = b2[n * nc:(n + 1) * nc, :LANES]
                binp[k0 + 1, seq_rows(n), :] = b2[n * nc:(n + 1) * nc, LANES:]
    lLr, lLi = pows[L]
    lam_r = [lanes(lLr, k) for k in range(hp)]
    lam_i = [lanes(lLi, k) for k in range(hp)]
    x0r = sre_ref[...]
    x0i = sim_ref[...]

    if nc == 1:
        for k in range(hp):
            xr, xi = lanes(x0r, k), lanes(x0i, k)
            xsb[:, k * LANES:(k + 1) * LANES] = xr.astype(BF16)
            xsb[:, (hp + k) * LANES:(hp + k + 1) * LANES] = xi.astype(BF16)
            ore_ref[:, k * LANES:(k + 1) * LANES] = xr * lam_r[k] - xi * lam_i[k] + binp[k]
            oim_ref[:, k * LANES:(k + 1) * LANES] = xr * lam_i[k] + xi * lam_r[k] + binp[hp + k]
    else:
        def step(c, carry):
            rows = pl.ds(pl.multiple_of(c * slots, slots), n_seq)
            new = []
            for k in range(hp):
                xr, xi = carry[k], carry[hp + k]
                xst[k, rows, :] = xr
                xst[hp + k, rows, :] = xi
                new.append((xr * lam_r[k] - xi * lam_i[k] + binp[k, rows, :],
                            xr * lam_i[k] + xi * lam_r[k] + binp[hp + k, rows, :]))
            return tuple(r for r, _ in new) + tuple(i for _, i in new)

        init = tuple(lanes(x0r, k) for k in range(hp)) + tuple(lanes(x0i, k) for k in range(hp))
        fin = lax.fori_loop(0, nc, step, init, unroll=4)
        ore_ref[...] = jnp.concatenate(fin[:hp], axis=1)
        oim_ref[...] = jnp.concatenate(fin[hp:], axis=1)
        for k in range(2 * hp):
            for n in range(n_seq):
                xsb[n * nc:(n + 1) * nc, k * LANES:(k + 1) * LANES] = xst[k, seq_rows(n), :].astype(BF16)

    dsk = d_ref[...]
    for c0 in range(0, L * LANES, ct):
        kmax = c0 + ct
        y = jnp.dot(xcat[:, :kmax], tmat[:kmax, c0:c0 + ct], preferred_element_type=F32)
        y = y + lax.dot_general(xsb[...], pmat[c0:c0 + ct, :], (((1,), (1,)), ((), ())),
                                preferred_element_type=F32)
        for t in range(c0 // LANES, (c0 + ct) // LANES):
            yt = y[:, t * LANES - c0:(t + 1) * LANES - c0] + dsk * u_ref[pos(t), :]
            z_ref[pos(t), :] = jax.nn.gelu(yt)


def _s5_lane_rows(a):
    g, p = a.shape
    return a.reshape(g // S5_GPL, 1, S5_GPL * p)


def _s5_tiles(a):
    g, c, p = a.shape
    return jnp.tile(a.reshape(g // S5_GPL, S5_GPL * c, p), (1, 1, S5_GPL))


def s5_core(u_lg, z_prev, row_start, n_seq, t_len, st_re, st_im,
            a_re, a_im, log_dt, b_re, b_im, c_re, c_im, d_skip):
    nlg, m, _ = u_lg.shape
    g, p = a_re.shape
    H = S5_HALF
    L = S5_CHUNK if t_len % S5_CHUNK == 0 else 1
    assert L > 1 or t_len == 1
    nc = t_len // L
    rows = n_seq * nc
    assert row_start % (rows * L) == 0 and m % L == 0
    rb = row_start // (rows * L)
    lane_args = [_s5_lane_rows(a_re), _s5_lane_rows(a_im),
                 _s5_lane_rows(jnp.broadcast_to(log_dt[:, None], (g, p)))]
    tile_args = [_s5_tiles(jnp.swapaxes(b_re, 1, 2)), _s5_tiles(jnp.swapaxes(b_im, 1, 2)),
                 _s5_tiles(c_re), _s5_tiles(c_im)]
    d_arg = d_skip.reshape(nlg, 1, LANES)
    lane_spec = pl.BlockSpec((None, 1, H), lambda j: (j, 0, 0))
    tile_spec = pl.BlockSpec((None, LANES, H), lambda j: (j, 0, 0))
    d_spec = pl.BlockSpec((None, 1, LANES), lambda j: (j, 0, 0))
    sublanes = 8
    plane_rows = rows if nc == 1 else nc * (-(-n_seq // sublanes) * sublanes)
    act_shape = (nlg, m, LANES)
    act_spec = pl.BlockSpec((None, rows * L, LANES), lambda j: (j, rb, 0))
    st_spec = pl.BlockSpec((n_seq, H), lambda j: (0, j))
    st_shape = jax.ShapeDtypeStruct((n_seq, g * p), F32)
    in_specs = [act_spec, st_spec, st_spec] + [lane_spec] * 3 + [tile_spec] * 4 + [d_spec]
    args = [u_lg.reshape(act_shape), st_re, st_im, *lane_args, *tile_args, d_arg]
    aliases = {}
    if z_prev is not None:
        in_specs.append(pl.BlockSpec(memory_space=pl.ANY))
        args.append(z_prev.reshape(act_shape))
        aliases = {len(args) - 1: 0}
    kern = functools.partial(_s5_kernel, L=L, nc=nc, n_seq=n_seq, aliased=z_prev is not None)
    z, o_re, o_im = pl.pallas_call(
        kern,
        grid=(nlg,),
        in_specs=in_specs,
        out_specs=[act_spec, st_spec, st_spec],
        out_shape=[jax.ShapeDtypeStruct(act_shape, F32), st_shape, st_shape],
        scratch_shapes=[pltpu.VMEM((L * LANES, L * LANES), BF16),
                        pltpu.VMEM((L * LANES, 2 * H), BF16),
                        pltpu.VMEM((L * LANES, 2 * H), BF16),
                        pltpu.VMEM((2 * H // LANES, plane_rows, LANES), F32),
                        pltpu.VMEM((2 * H // LANES, plane_rows, LANES), F32),
                        pltpu.VMEM((rows, L * LANES), BF16),
                        pltpu.VMEM((rows, 2 * H), BF16)],
        input_output_aliases=aliases,
        compiler_params=_params(("arbitrary",)),
        name="s5_core",
    )(*args)
    return z.reshape(nlg, m, LANES), o_re, o_im


def _conv_prompt_kernel(gb_ref, gc_ref, v_ref, buf_ref, cw_ref, z_ref, nb_ref):
    pre = gc_ref[...] * v_ref[...]
    t = pre.shape[0]
    buf = buf_ref[...]
    row = lax.broadcasted_iota(jnp.int32, pre.shape, 0)
    sh1 = jnp.where(row == 0, buf[1:2, :], pltpu.roll(pre, 1, 0))
    sh2 = pltpu.roll(pre, 2, 0)
    sh2 = jnp.where(row == 0, buf[0:1, :], jnp.where(row == 1, buf[1:2, :], sh2))
    cw = cw_ref[...]
    conv = cw[0:1, :] * sh2 + cw[1:2, :] * sh1 + cw[2:3, :] * pre
    z_ref[...] = (gb_ref[...] * conv).astype(z_ref.dtype)
    nb_ref[...] = pre[t - 2:t, :]


def conv_prompt(proj, n_seq, t_len, buf, conv_w):
    d = proj.shape[1] // 3
    tc = _pick(d, (256, 128))
    ncb = d // tc
    return pl.pallas_call(
        _conv_prompt_kernel,
        grid=(n_seq, ncb),
        in_specs=[pl.BlockSpec((t_len, tc), lambda n, j: (n, j)),
                  pl.BlockSpec((t_len, tc), lambda n, j: (n, ncb + j)),
                  pl.BlockSpec((t_len, tc), lambda n, j: (n, 2 * ncb + j)),
                  pl.BlockSpec((None, 2, tc), lambda n, j: (n, 0, j)),
                  pl.BlockSpec((3, tc), lambda n, j: (0, j))],
        out_specs=[pl.BlockSpec((t_len, tc), lambda n, j: (n, j)),
                   pl.BlockSpec((None, 2, tc), lambda n, j: (n, 0, j))],
        out_shape=[jax.ShapeDtypeStruct((proj.shape[0], d), BF16),
                   jax.ShapeDtypeStruct((n_seq, 2, d), F32)],
        compiler_params=_params(("arbitrary", "arbitrary")),
        name="conv_prompt",
    )(proj, proj, proj, buf, conv_w)


def _conv_step_kernel(gb_ref, gc_ref, v_ref, b0_ref, b1_ref, cw_ref, zprev_ref, z_ref, n0_ref, n1_ref):
    pre = gc_ref[...] * v_ref[...]
    b0 = b0_ref[...]
    b1 = b1_ref[...]
    cw = cw_ref[...]
    conv = cw[0:1, :] * b0 + cw[1:2, :] * b1 + cw[2:3, :] * pre
    z_ref[...] = (gb_ref[...] * conv).astype(z_ref.dtype)
    n0_ref[...] = b1
    n1_ref[...] = pre


def conv_step(proj, z_prev, row_start, n_seq, buf, conv_w):
    d = proj.shape[1] // 3
    tc = _pick(d, (512, 256, 128))
    ncb = d // tc
    rb = row_start // n_seq
    buf2 = buf.reshape(n_seq, 2 * d)
    row_spec = lambda off: pl.BlockSpec((n_seq, tc), lambda j: (rb, off * ncb + j))
    b_spec = lambda off: pl.BlockSpec((n_seq, tc), lambda j: (0, off * ncb + j))
    o_spec = pl.BlockSpec((n_seq, tc), lambda j: (0, j))
    z, n0, n1 = pl.pallas_call(
        _conv_step_kernel,
        grid=(ncb,),
        in_specs=[row_spec(0), row_spec(1), row_spec(2), b_spec(0), b_spec(1),
                  pl.BlockSpec((3, tc), lambda j: (0, j)),
                  pl.BlockSpec(memory_space=pl.ANY)],
        out_specs=[pl.BlockSpec((n_seq, tc), lambda j: (rb, j)), o_spec, o_spec],
        out_shape=[jax.ShapeDtypeStruct(z_prev.shape, BF16),
                   jax.ShapeDtypeStruct((n_seq, d), F32),
                   jax.ShapeDtypeStruct((n_seq, d), F32)],
        input_output_aliases={6: 0},
        compiler_params=_params(("arbitrary",)),
        name="conv_step",
    )(proj, proj, proj, buf2, buf2, conv_w, z_prev)
    return z, jnp.stack([n0, n1], axis=1)


def _hgrn_lower_bound(logit_ref, layer):
    lg = logit_ref[...]
    rows = [lg[i:i + 1, :] for i in range(lg.shape[0])]
    mx = functools.reduce(jnp.maximum, rows)
    es = [jnp.exp(r - mx) for r in rows]
    tot = functools.reduce(lambda a, b: a + b, es)
    part = functools.reduce(lambda a, b: a + b, es[1:layer + 1])
    return part / tot


def _hgrn_gate_out(o, gn, g):
    o = o * lax.rsqrt(jnp.mean(o * o, axis=-1, keepdims=True) + RMS_EPS)
    return o * gn * jax.nn.sigmoid(g)


def _hgrn_prompt_kernel(q_ref, f_ref, v_ref, g_ref, lgt_ref, gn_ref, s0_ref, z_ref, so_ref, st_ref,
                        *, layer, hb, tb, L):
    tblk = pl.program_id(2)
    lb = _hgrn_lower_bound(lgt_ref, layer)
    gn = gn_ref[...]

    @pl.when(tblk == 0)
    def _():
        for h in range(hb):
            st_ref[h] = s0_ref[0, h].T

    ri = lax.broadcasted_iota(jnp.int32, (L, L), 0)
    ci = lax.broadcasted_iota(jnp.int32, (L, L), 1)
    causal = ci <= ri
    tri = causal.astype(F32)
    mid = L // 2

    def chunk(c, carry):
        r0 = pl.multiple_of(c * L, L)
        fpre = f_ref[pl.ds(r0, L), :]
        f = lb + (1.0 - lb) * jax.nn.sigmoid(fpre)
        logf = jnp.log(f)
        kk = 1.0 - f
        b = jnp.dot(tri, logf, preferred_element_type=F32, precision=lax.Precision.HIGHEST)
        q = jax.nn.silu(q_ref[pl.ds(r0, L), :])
        v = v_ref[pl.ds(r0, L), :]
        g = g_ref[pl.ds(r0, L), :]
        b_end = b[L - 1:L, :]
        b_mid = b[mid - 1:mid, :]
        qt = (q * jnp.exp(b - b_mid)).astype(BF16)
        kt = (kk * jnp.exp(b_mid - b)).astype(BF16)
        qd = (q * jnp.exp(b)).astype(BF16)
        kd = (kk * jnp.exp(b_end - b)).astype(BF16)
        vb = v.astype(BF16)
        dec = jnp.exp(b_end)
        outs = []
        for h in range(hb):
            sl = slice(h * HG_DK, (h + 1) * HG_DK)
            att = lax.dot_general(qt[:, sl], kt[:, sl], (((1,), (1,)), ((), ())),
                                  preferred_element_type=F32)
            att = jnp.where(causal, att, 0.0).astype(BF16)
            s_t = st_ref[h]
            o = jnp.dot(att, vb[:, sl], preferred_element_type=F32)
            o = o + lax.dot_general(qd[:, sl], s_t.astype(BF16), (((1,), (1,)), ((), ())),
                                    preferred_element_type=F32)
            upd = lax.dot_general(vb[:, sl], kd[:, sl], (((0,), (0,)), ((), ())),
                                  preferred_element_type=F32)
            st_ref[h] = s_t * dec[:, sl] + upd
            outs.append(_hgrn_gate_out(o, gn[:, sl], g[:, sl]))
        z_ref[pl.ds(r0, L), :] = jnp.concatenate(outs, axis=1).astype(z_ref.dtype)
        return carry

    lax.fori_loop(0, tb // L, chunk, 0, unroll=2)

    @pl.when(tblk == pl.num_programs(2) - 1)
    def _():
        for h in range(hb):
            so_ref[0, h] = st_ref[h].T


def hgrn_prompt(proj, n_seq, t_len, s0, lb_logits, gnorm, layer):
    d = proj.shape[1] // 4
    heads = d // HG_DK
    hb = _pick(heads, (8, 4, 2, 1))
    nhb = heads // hb
    L = HG_CHUNK if t_len % HG_CHUNK == 0 else t_len
    tb = _pick(t_len, (512, 256, 128, 64))
    if tb % L:
        tb = t_len
    ntb = t_len // tb
    w = hb * HG_DK
    sec = lambda k: pl.BlockSpec((tb, w), lambda n, j, t: (n * ntb + t, k * nhb + j))
    kern = functools.partial(_hgrn_prompt_kernel, layer=layer, hb=hb, tb=tb, L=L)
    return pl.pallas_call(
        kern,
        grid=(n_seq, nhb, ntb),
        in_specs=[sec(0), sec(1), sec(2), sec(3),
                  pl.BlockSpec((lb_logits.shape[0], w), lambda n, j, t: (0, j)),
                  pl.BlockSpec((1, w), lambda n, j, t: (0, j)),
                  pl.BlockSpec((1, hb, HG_DK, HG_DK), lambda n, j, t: (n, j, 0, 0))],
        out_specs=[pl.BlockSpec((tb, w), lambda n, j, t: (n * ntb + t, j)),
                   pl.BlockSpec((1, hb, HG_DK, HG_DK), lambda n, j, t: (n, j, 0, 0))],
        out_shape=[jax.ShapeDtypeStruct((proj.shape[0], d), BF16),
                   jax.ShapeDtypeStruct(s0.shape, F32)],
        scratch_shapes=[pltpu.VMEM((hb, HG_DK, HG_DK), F32)],
        compiler_params=_params(("arbitrary", "arbitrary", "arbitrary")),
        name="hgrn_prompt",
    )(proj, proj, proj, proj, lb_logits, gnorm.reshape(1, d), s0)


def _hgrn_step_kernel(q_ref, f_ref, v_ref, g_ref, lgt_ref, gn_ref, s_ref, zprev_ref, z_ref, so_ref,
                      *, layer, nb):
    lb = _hgrn_lower_bound(lgt_ref, layer)
    f = lb + (1.0 - lb) * jax.nn.sigmoid(f_ref[...])
    q = jax.nn.silu(q_ref[...])
    v = v_ref[...]
    f_t = f.T
    k_t = 1.0 - f_t
    q_t = q.T
    outs = []
    for i in range(nb):
        s_new = f_t[:, i:i + 1] * s_ref[i, 0] + k_t[:, i:i + 1] * v[i:i + 1, :]
        so_ref[i, 0] = s_new
        outs.append(jnp.sum(q_t[:, i:i + 1] * s_new, axis=0, keepdims=True))
    o = jnp.concatenate(outs, axis=0)
    z_ref[...] = _hgrn_gate_out(o, gn_ref[...], g_ref[...]).astype(z_ref.dtype)


def hgrn_step(proj, z_prev, row_start, n_seq, s0, lb_logits, gnorm, layer):
    d = proj.shape[1] // 4
    heads = d // HG_DK
    nb = _pick(n_seq, (32, 16, 8))
    nnb = n_seq // nb
    rb = row_start // nb
    sec = lambda k: pl.BlockSpec((nb, HG_DK), lambda h, i: (rb + i, k * heads + h))
    s_spec = pl.BlockSpec((nb, 1, HG_DK, HG_DK), lambda h, i: (i, h, 0, 0))
    kern = functools.partial(_hgrn_step_kernel, layer=layer, nb=nb)
    return pl.pallas_call(
        kern,
        grid=(heads, nnb),
        in_specs=[sec(0), sec(1), sec(2), sec(3),
                  pl.BlockSpec((lb_logits.shape[0], HG_DK), lambda h, i: (0, h)),
                  pl.BlockSpec((1, HG_DK), lambda h, i: (0, h)),
                  s_spec,
                  pl.BlockSpec(memory_space=pl.ANY)],
        out_specs=[pl.BlockSpec((nb, HG_DK), lambda h, i: (rb + i, h)), s_spec],
        out_shape=[jax.ShapeDtypeStruct(z_prev.shape, BF16),
                   jax.ShapeDtypeStruct(s0.shape, F32)],
        input_output_aliases={7: 0},
        compiler_params=_params(("arbitrary", "arbitrary")),
        name="hgrn_step",
    )(proj, proj, proj, proj, lb_logits, gnorm.reshape(1, d), s0, z_prev)


def _row_tile(m, candidates):
    return _pick(m, candidates + (256, 128, 64, 32, 16))


def kernel(x_prompt, x_sample, state_s5_re, state_s5_im, state_conv, state_hgrn,
           norm_mix, norm_ffn, norm_final,
           s5_a_re, s5_a_im, s5_log_dt, s5_b_re, s5_b_im, s5_c_re, s5_c_im, s5_d, s5_w_glu,
           conv_w_in, conv_w, conv_w_out,
           hgrn_w_in, hgrn_lb_logits, hgrn_gnorm, hgrn_w_out,
           ffn_w_in, ffn_w_out):
    nb, t_len, d = x_prompt.shape
    ns = x_sample.shape[0]
    assert x_sample.shape[1] == 1
    mp = nb * t_len
    m = mp + ns
    depth = norm_mix.shape[0]
    g, p = s5_a_re.shape[1], s5_a_re.shape[2]
    gp = g * p
    dff = ffn_w_out.shape[1]
    heads = d // HG_DK
    tm_big = _row_tile(m, (1040,))
    tm_mid = _row_tile(m, (640,))
    tm_glu = _row_tile(m, (520,))
    tn_big = _pick(d, (512, 256, 128))
    tn_ff = _pick(dff, (256, 128))
    kk_ff = dff // 2 if (dff // 2) % LANES == 0 else dff

    h = jnp.concatenate([x_prompt.reshape(mp, d), x_sample.reshape(ns, d)], axis=0)
    z_state = jnp.zeros((nb, gp), F32)

    p_re, p_im, s_re, s_im, p_conv, s_conv, p_hg, s_hg = [], [], [], [], [], [], [], []
    for layer in range(depth):
        kind, j = layer % N_MIXERS, layer // N_MIXERS
        if kind == 0:
            u = rmsnorm_rows(h, norm_mix[layer], 0, m, F32, lane_groups=True)
            prm = (s5_a_re[j], s5_a_im[j], s5_log_dt[j], s5_b_re[j], s5_b_im[j],
                   s5_c_re[j], s5_c_im[j], s5_d[j])
            z, pre_, pim_ = s5_core(u, None, 0, nb, t_len, z_state, z_state, *prm)
            z, sre_, sim_ = s5_core(u, z, mp, ns, 1, state_s5_re[j].reshape(ns, gp),
                                    state_s5_im[j].reshape(ns, gp), *prm)
            p_re.append(pre_)
            p_im.append(pim_)
            s_re.append(sre_)
            s_im.append(sim_)
            tn = _pick(d, (256, 128))
            h = dense(z, [(s5_w_glu, j, 0, 0), (s5_w_glu, j, 0, d // tn)], d, epi=_epi_glu_res,
                      out_dtype=F32, res=h, tm=tm_glu, tn=tn, x_lg=True, name="s5_glu")
        elif kind == 1:
            xn = rmsnorm_rows(h, norm_mix[layer], 0, m, BF16)
            proj = dense(xn, [(conv_w_in, j, 0, 0)], 3 * d, epi=_epi_plain, out_dtype=F32,
                         tm=tm_big, tn=tn_big, name="conv_in")
            z, pb = conv_prompt(proj, nb, t_len, jnp.zeros((nb, 2, d), F32), conv_w[j])
            z, sb = conv_step(proj, z, mp, ns, state_conv[j], conv_w[j])
            p_conv.append(pb)
            s_conv.append(sb)
            h = dense(z, [(conv_w_out, j, 0, 0)], d, epi=_epi_res, out_dtype=F32, res=h,
                      tm=tm_big, tn=tn_big, name="conv_out")
        else:
            xn = rmsnorm_rows(h, norm_mix[layer], 0, m, BF16)
            proj = dense(xn, [(hgrn_w_in, j, 0, 0)], 4 * d, epi=_epi_plain, out_dtype=F32,
                         tm=tm_big, tn=tn_big, name="hgrn_in")
            z, ps = hgrn_prompt(proj, nb, t_len, jnp.zeros((nb, heads, HG_DK, HG_DK), F32),
                                hgrn_lb_logits, hgrn_gnorm[j], layer)
            z, ss = hgrn_step(proj, z, mp, ns, state_hgrn[j], hgrn_lb_logits, hgrn_gnorm[j], layer)
            p_hg.append(ps)
            s_hg.append(ss)
            h = dense(z, [(hgrn_w_out, j, 0, 0)], d, epi=_epi_res, out_dtype=F32, res=h,
                      tm=tm_big, tn=tn_big, name="hgrn_out")
        xn = rmsnorm_rows(h, norm_ffn[layer], 0, m, BF16)
        hf = dense(xn, [(ffn_w_in, layer, 0, 0), (ffn_w_in, layer, 0, dff // tn_ff)], dff,
                   epi=_epi_swiglu, out_dtype=BF16, tm=tm_big, tn=tn_ff, name="ffn_in")
        for kb in range(dff // kk_ff):
            h = dense(hf, [(ffn_w_out, layer, kb, 0)], d, epi=_epi_res, out_dtype=F32, res=h,
                      tm=tm_mid, tn=tn_big, kk=kk_ff, x_kblock=kb, name="ffn_out")

    y_prompt = rmsnorm_rows(h, norm_final, 0, mp, F32).reshape(nb, t_len, d)
    y_sample = rmsnorm_rows(h, norm_final, mp, ns, F32).reshape(ns, 1, d)
    return (y_prompt, y_sample,
            jnp.stack(p_re).reshape(-1, nb, g, p), jnp.stack(p_im).reshape(-1, nb, g, p),
            jnp.stack(p_conv), jnp.stack(p_hg),
            jnp.stack(s_re).reshape(-1, ns, g, p), jnp.stack(s_im).reshape(-1, ns, g, p),
            jnp.stack(s_conv), jnp.stack(s_hg))
```

```python
import functools
import math

import jax
import jax.numpy as jnp
from jax import lax
from jax.experimental import pallas as pl
from jax.experimental.pallas import tpu as pltpu

F32 = jnp.float32
BF16 = jnp.bfloat16

RMS_EPS = 1e-6
N_MIXERS = 3
S5_GROUP = 16
S5_STATE = 64
S5_CHUNK = 8
HG_DK = 128
HG_CHUNK = 64
LANES = 128
S5_GPL = LANES // S5_GROUP
S5_HALF = S5_GPL * S5_STATE
VMEM_LIMIT = 56 * 1024 * 1024


def _params(sem):
    return pltpu.CompilerParams(dimension_semantics=sem, vmem_limit_bytes=VMEM_LIMIT)


def _pick(dim, candidates):
    for c in candidates:
        if dim % c == 0:
            return c
    return dim


def _rms(x, w):
    return x * lax.rsqrt(jnp.mean(x * x, axis=-1, keepdims=True) + RMS_EPS) * w


def _rmsnorm_kernel(x_ref, w_ref, o_ref, *, lane_groups):
    tm = x_ref.shape[0]
    rc = _pick(tm, (64, 32, 16, 8))
    for r in range(0, tm, rc):
        y = _rms(x_ref[r:r + rc, :], w_ref[...]).astype(o_ref.dtype)
        if lane_groups:
            for g in range(o_ref.shape[0]):
                o_ref[g, r:r + rc, :] = y[:, g * LANES:(g + 1) * LANES]
        else:
            o_ref[r:r + rc, :] = y


def rmsnorm_rows(x, w, row_start, n_rows, out_dtype, lane_groups=False):
    d = x.shape[1]
    tm = _pick(math.gcd(row_start, n_rows) if row_start else n_rows, (320, 256, 128, 64, 32, 16, 8))
    off = row_start // tm
    if lane_groups:
        out_spec = pl.BlockSpec((d // LANES, tm, LANES), lambda i: (0, i, 0))
        out_shape = jax.ShapeDtypeStruct((d // LANES, n_rows, LANES), out_dtype)
    else:
        out_spec = pl.BlockSpec((tm, d), lambda i: (i, 0))
        out_shape = jax.ShapeDtypeStruct((n_rows, d), out_dtype)
    return pl.pallas_call(
        functools.partial(_rmsnorm_kernel, lane_groups=lane_groups),
        grid=(n_rows // tm,),
        in_specs=[pl.BlockSpec((tm, d), lambda i: (i + off, 0)),
                  pl.BlockSpec((1, d), lambda i: (0, 0))],
        out_specs=out_spec,
        out_shape=out_shape,
        compiler_params=_params(("arbitrary",)),
        name="rmsnorm",
    )(x, w.reshape(1, d))


def _epi_plain(accs, res):
    return accs[0]


def _epi_res(accs, res):
    return res + accs[0]


def _epi_glu_res(accs, res):
    return res + accs[0] * jax.nn.sigmoid(accs[1])


def _epi_swiglu(accs, res):
    return jax.nn.silu(accs[0]) * accs[1]


def _dense_kernel(*refs, n_w, has_res, has_norm, emit_bf16, epi):
    it = iter(refs)
    x_ref = next(it)
    nw_ref = next(it) if has_norm else None
    w_refs = [next(it) for _ in range(n_w)]
    res_ref = next(it) if has_res else None
    o_ref = next(it)
    ob_ref = next(it) if emit_bf16 else None
    wb_refs = [next(it) for _ in range(n_w)]
    rinv_ref = next(it) if has_norm else None
    j = pl.program_id(0)
    i = pl.program_id(1)

    @pl.when(i == 0)
    def _cast():
        for w_ref, wb_ref in zip(w_refs, wb_refs):
            kk = w_ref.shape[0]
            rc = kk // 8
            for r in range(0, kk, rc):
                w = w_ref[r:r + rc, :]
                if has_norm:
                    gain = nw_ref[r:r + rc, :]
                    w = w * jnp.concatenate([gain] * (w.shape[1] // LANES), axis=1)
                wb_ref[r:r + rc, :] = w.astype(BF16)

    if has_norm:
        @pl.when(j == 0)
        def _row_scale():
            tm, k = x_ref.shape
            rc = _pick(tm, (208, 128, 64, 32, 16))
            for r in range(0, tm, rc):
                xf = x_ref[r:r + rc, :].astype(F32)
                ssq = jnp.sum(xf * xf, axis=-1, keepdims=True)
                rinv_ref[i, r:r + rc, :] = jnp.broadcast_to(lax.rsqrt(ssq / k + RMS_EPS), (rc, LANES))

    xb = x_ref[...]
    accs = [jnp.dot(xb, wb[...], preferred_element_type=F32) for wb in wb_refs]
    if has_norm:
        rinv = rinv_ref[i]
        reps = accs[0].shape[1] // LANES
        scale = jnp.concatenate([rinv] * reps, axis=1) if reps > 1 else rinv
        accs = [a * scale for a in accs]
    res = res_ref[...] if has_res else None
    out = epi(accs, res)
    o_ref[...] = out.astype(o_ref.dtype)
    if emit_bf16:
        ob_ref[...] = out.astype(BF16)


def dense(x, ws, n_out, *, epi, out_dtype, tm, tn, res=None, norm_w=None, emit_bf16=False,
          kk=None, x_kblock=0, name="dense"):
    m = x.shape[0]
    kk = kk or x.shape[1]
    assert m % tm == 0 and n_out % tn == 0
    assert norm_w is None or kk == x.shape[1]
    n_m = m // tm
    in_specs = [pl.BlockSpec((tm, kk), lambda j, i: (i, x_kblock))]
    args = [x]
    if norm_w is not None:
        in_specs.append(pl.BlockSpec((kk, LANES), lambda j, i: (0, 0)))
        args.append(jnp.broadcast_to(norm_w[:, None], (kk, LANES)))
    for w, lyr, rb, cb in ws:
        in_specs.append(pl.BlockSpec((None, kk, tn), lambda j, i, lyr=lyr, rb=rb, cb=cb: (lyr, rb, cb + j)))
        args.append(w)
    if res is not None:
        in_specs.append(pl.BlockSpec((tm, tn), lambda j, i: (i, j)))
        args.append(res)
    out_spec = pl.BlockSpec((tm, tn), lambda j, i: (i, j))
    out_specs = [out_spec]
    out_shape = [jax.ShapeDtypeStruct((m, n_out), out_dtype)]
    if emit_bf16:
        out_specs.append(out_spec)
        out_shape.append(jax.ShapeDtypeStruct((m, n_out), BF16))
    scratch = [pltpu.VMEM((kk, tn), BF16) for _ in ws]
    if norm_w is not None:
        scratch.append(pltpu.VMEM((n_m, tm, LANES), F32))
    kern = functools.partial(_dense_kernel, n_w=len(ws), has_res=res is not None,
                             has_norm=norm_w is not None, emit_bf16=emit_bf16, epi=epi)
    outs = pl.pallas_call(
        kern,
        grid=(n_out // tn, n_m),
        in_specs=in_specs,
        out_specs=out_specs,
        out_shape=out_shape,
        scratch_shapes=scratch,
        compiler_params=_params(("arbitrary", "arbitrary")),
        name=name,
    )(*args)
    return outs if emit_bf16 else outs[0]


def _planes_to_rows_kernel(x_ref, o_ref):
    for g in range(x_ref.shape[0]):
        o_ref[:, g * LANES:(g + 1) * LANES] = x_ref[g].astype(o_ref.dtype)


def planes_to_rows(x_lg, out_dtype):
    nlg, m, _ = x_lg.shape
    tm = _pick(m, (320, 256, 128, 64, 32, 16))
    return pl.pallas_call(
        _planes_to_rows_kernel,
        grid=(m // tm,),
        in_specs=[pl.BlockSpec((nlg, tm, LANES), lambda i: (0, i, 0))],
        out_specs=pl.BlockSpec((tm, nlg * LANES), lambda i: (i, 0)),
        out_shape=jax.ShapeDtypeStruct((m, nlg * LANES), out_dtype),
        compiler_params=_params(("arbitrary",)),
        name="planes_to_rows",
    )(x_lg)


def _s5_kernel(*refs, L, nc, n_seq, aliased):
    (u_ref, sre_ref, sim_ref, ar_ref, ai_ref, ldt_ref, br_ref, bi_ref, cr_ref, ci_ref, d_ref) = refs[:11]
    z_ref, ore_ref, oim_ref, tmat, qmat, pmat, binp, xst, xcat, xsb = refs[11 + aliased:]
    H = S5_HALF

    ar = ar_ref[...]
    ai = ai_ref[...]
    dt = jnp.exp(ldt_ref[...])
    mag = jnp.exp(ar * dt)
    lr = mag * jnp.cos(ai * dt)
    li = mag * jnp.sin(ai * dt)
    den = ar * ar + ai * ai
    cor = ((lr - 1.0) * ar + li * ai) / den
    coi = (li * ar - (lr - 1.0) * ai) / den

    pows = [(jnp.ones_like(lr), jnp.zeros_like(lr))]
    for _ in range(L):
        pr, pi = pows[-1]
        pows.append((pr * lr - pi * li, pr * li + pi * lr))

    row_g = lax.broadcasted_iota(jnp.int32, (LANES, H), 0) // S5_GROUP
    lane_g = lax.broadcasted_iota(jnp.int32, (LANES, H), 1) // S5_STATE
    diag = row_g == lane_g
    br = jnp.where(diag, br_ref[...], 0.0)
    bi = jnp.where(diag, bi_ref[...], 0.0)
    cr = jnp.where(diag, cr_ref[...], 0.0)
    ci = jnp.where(diag, ci_ref[...], 0.0)
    a_r = br * cor - bi * coi
    a_i = br * coi + bi * cor
    a_cat = jnp.concatenate([a_r, a_i], axis=1).astype(BF16)
    zero_blk = jnp.zeros((LANES, LANES), BF16)
    for k in range(L + 1):
        pr, pi = pows[k]
        w_k = jnp.concatenate([cr * pr - ci * pi, -(cr * pi + ci * pr)], axis=1).astype(BF16)
        if k >= 1:
            pmat[(k - 1) * LANES:k * LANES, :] = w_k
        if k < L:
            d_k = lax.dot_general(a_cat, w_k, (((1,), (1,)), ((), ())),
                                  preferred_element_type=F32).astype(BF16)
            for s in range(L - k):
                t = s + k
                tmat[s * LANES:(s + 1) * LANES, t * LANES:(t + 1) * LANES] = d_k
            s = L - 1 - k
            qmat[s * LANES:(s + 1) * LANES, :] = jnp.concatenate(
                [a_r * pr - a_i * pi, a_r * pi + a_i * pr], axis=1).astype(BF16)
    for s in range(L):
        for t in range(s):
            tmat[s * LANES:(s + 1) * LANES, t * LANES:(t + 1) * LANES] = zero_blk

    hp = H // LANES
    lanes = lambda a, k: a[:, k * LANES:(k + 1) * LANES]
    ct = 2 * LANES if L % 2 == 0 else LANES

    n_rows = n_seq * nc
    pos = lambda s: pl.ds(s, n_rows, stride=L) if L > 1 else slice(None)

    for s in range(L):
        xcat[:, s * LANES:(s + 1) * LANES] = u_ref[pos(s), :].astype(BF16)

    slots = binp.shape[1] // nc
    seq_rows = lambda n: pl.ds(n, nc, stride=slots)
    for k0 in range(0, 2 * hp, 2):
        b2 = jnp.dot(xcat[...], qmat[:, k0 * LANES:(k0 + 2) * LANES], preferred_element_type=F32)
        if nc == 1:
            binp[k0] = b2[:, :LANES]
            binp[k0 + 1] = b2[:, LANES:]
        else:
            for n in range(n_seq):
                binp[k0, seq_rows(n), :] = b2[n * nc:(n + 1) * nc, :LANES]
                binp[k0 + 1, seq_rows(n), :] = b2[n * nc:(n + 1) * nc, LANES:]
    lLr, lLi = pows[L]
    lam_r = [lanes(lLr, k) for k in range(hp)]
    lam_i = [lanes(lLi, k) for k in range(hp)]
    x0r = sre_ref[...]
    x0i = sim_ref[...]

    if nc == 1:
        for k in range(hp):
            xr, xi = lanes(x0r, k), lanes(x0i, k)
            xsb[:, k * LANES:(k + 1) * LANES] = xr.astype(BF16)
            xsb[:, (hp + k) * LANES:(hp + k + 1) * LANES] = xi.astype(BF16)
            ore_ref[:, k * LANES:(k + 1) * LANES] = xr * lam_r[k] - xi * lam_i[k] + binp[k]
            oim_ref[:, k * LANES:(k + 1) * LANES] = xr * lam_i[k] + xi * lam_r[k] + binp[hp + k]
    else:
        def step(c, carry):
            rows = pl.ds(pl.multiple_of(c * slots, slots), n_seq)
            new = []
            for k in range(hp):
                xr, xi = carry[k], carry[hp + k]
                xst[k, rows, :] = xr
                xst[hp + k, rows, :] = xi
                new.append((xr * lam_r[k] - xi * lam_i[k] + binp[k, rows, :],
                            xr * lam_i[k] + xi * lam_r[k] + binp[hp + k, rows, :]))
            return tuple(r for r, _ in new) + tuple(i for _, i in new)

        init = tuple(lanes(x0r, k) for k in range(hp)) + tuple(lanes(x0i, k) for k in range(hp))
        fin = lax.fori_loop(0, nc, step, init, unroll=4)
        ore_ref[...] = jnp.concatenate(fin[:hp], axis=1)
        oim_ref[...] = jnp.concatenate(fin[hp:], axis=1)
        for k in range(2 * hp):
            for n in range(n_seq):
                xsb[n * nc:(n + 1) * nc, k * LANES:(k + 1) * LANES] = xst[k, seq_rows(n), :].astype(BF16)

    dsk = d_ref[...]
    for c0 in range(0, L * LANES, ct):
        kmax = c0 + ct
        y = jnp.dot(xcat[:, :kmax], tmat[:kmax, c0:c0 + ct], preferred_element_type=F32)
        y = y + lax.dot_general(xsb[...], pmat[c0:c0 + ct, :], (((1,), (1,)), ((), ())),
                                preferred_element_type=F32)
        for t in range(c0 // LANES, (c0 + ct) // LANES):
            yt = y[:, t * LANES - c0:(t + 1) * LANES - c0] + dsk * u_ref[pos(t), :]
            z_ref[pos(t), :] = jax.nn.gelu(yt)


def _s5_lane_rows(a):
    g, p = a.shape
    return a.reshape(g // S5_GPL, 1, S5_GPL * p)


def _s5_tiles(a):
    g, c, p = a.shape
    return jnp.tile(a.reshape(g // S5_GPL, S5_GPL * c, p), (1, 1, S5_GPL))


def s5_core(u_lg, z_prev, row_start, n_seq, t_len, st_re, st_im,
            a_re, a_im, log_dt, b_re, b_im, c_re, c_im, d_skip):
    nlg, m, _ = u_lg.shape
    g, p = a_re.shape
    H = S5_HALF
    L = S5_CHUNK if t_len % S5_CHUNK == 0 else 1
    assert L > 1 or t_len == 1
    nc = t_len // L
    rows = n_seq * nc
    assert row_start % (rows * L) == 0 and m % L == 0
    rb = row_start // (rows * L)
    lane_args = [_s5_lane_rows(a_re), _s5_lane_rows(a_im),
                 _s5_lane_rows(jnp.broadcast_to(log_dt[:, None], (g, p)))]
    tile_args = [_s5_tiles(jnp.swapaxes(b_re, 1, 2)), _s5_tiles(jnp.swapaxes(b_im, 1, 2)),
                 _s5_tiles(c_re), _s5_tiles(c_im)]
    d_arg = d_skip.reshape(nlg, 1, LANES)
    lane_spec = pl.BlockSpec((None, 1, H), lambda j: (j, 0, 0))
    tile_spec = pl.BlockSpec((None, LANES, H), lambda j: (j, 0, 0))
    d_spec = pl.BlockSpec((None, 1, LANES), lambda j: (j, 0, 0))
    sublanes = 8
    plane_rows = rows if nc == 1 else nc * (-(-n_seq // sublanes) * sublanes)
    act_shape = (nlg, m, LANES)
    act_spec = pl.BlockSpec((None, rows * L, LANES), lambda j: (j, rb, 0))
    st_spec = pl.BlockSpec((n_seq, H), lambda j: (0, j))
    st_shape = jax.ShapeDtypeStruct((n_seq, g * p), F32)
    in_specs = [act_spec, st_spec, st_spec] + [lane_spec] * 3 + [tile_spec] * 4 + [d_spec]
    args = [u_lg.reshape(act_shape), st_re, st_im, *lane_args, *tile_args, d_arg]
    aliases = {}
    if z_prev is not None:
        in_specs.append(pl.BlockSpec(memory_space=pl.ANY))
        args.append(z_prev.reshape(act_shape))
        aliases = {len(args) - 1: 0}
    kern = functools.partial(_s5_kernel, L=L, nc=nc, n_seq=n_seq, aliased=z_prev is not None)
    z, o_re, o_im = pl.pallas_call(
        kern,
        grid=(nlg,),
        in_specs=in_specs,
        out_specs=[act_spec, st_spec, st_spec],
        out_shape=[jax.ShapeDtypeStruct(act_shape, F32), st_shape, st_shape],
        scratch_shapes=[pltpu.VMEM((L * LANES, L * LANES), BF16),
                        pltpu.VMEM((L * LANES, 2 * H), BF16),
                        pltpu.VMEM((L * LANES, 2 * H), BF16),
                        pltpu.VMEM((2 * H // LANES, plane_rows, LANES), F32),
                        pltpu.VMEM((2 * H // LANES, plane_rows, LANES), F32),
                        pltpu.VMEM((rows, L * LANES), BF16),
                        pltpu.VMEM((rows, 2 * H), BF16)],
        input_output_aliases=aliases,
        compiler_params=_params(("arbitrary",)),
        name="s5_core",
    )(*args)
    return z.reshape(nlg, m, LANES), o_re, o_im


def _conv_prompt_kernel(gb_ref, gc_ref, v_ref, buf_ref, cw_ref, z_ref, nb_ref):
    pre = gc_ref[...] * v_ref[...]
    t = pre.shape[0]
    buf = buf_ref[...]
    row = lax.broadcasted_iota(jnp.int32, pre.shape, 0)
    sh1 = jnp.where(row == 0, buf[1:2, :], pltpu.roll(pre, 1, 0))
    sh2 = pltpu.roll(pre, 2, 0)
    sh2 = jnp.where(row == 0, buf[0:1, :], jnp.where(row == 1, buf[1:2, :], sh2))
    cw = cw_ref[...]
    conv = cw[0:1, :] * sh2 + cw[1:2, :] * sh1 + cw[2:3, :] * pre
    z_ref[...] = (gb_ref[...] * conv).astype(z_ref.dtype)
    nb_ref[...] = pre[t - 2:t, :]


def conv_prompt(proj, n_seq, t_len, buf, conv_w):
    d = proj.shape[1] // 3
    tc = _pick(d, (256, 128))
    ncb = d // tc
    return pl.pallas_call(
        _conv_prompt_kernel,
        grid=(n_seq, ncb),
        in_specs=[pl.BlockSpec((t_len, tc), lambda n, j: (n, j)),
                  pl.BlockSpec((t_len, tc), lambda n, j: (n, ncb + j)),
                  pl.BlockSpec((t_len, tc), lambda n, j: (n, 2 * ncb + j)),
                  pl.BlockSpec((None, 2, tc), lambda n, j: (n, 0, j)),
                  pl.BlockSpec((3, tc), lambda n, j: (0, j))],
        out_specs=[pl.BlockSpec((t_len, tc), lambda n, j: (n, j)),
                   pl.BlockSpec((None, 2, tc), lambda n, j: (n, 0, j))],
        out_shape=[jax.ShapeDtypeStruct((proj.shape[0], d), BF16),
                   jax.ShapeDtypeStruct((n_seq, 2, d), F32)],
        compiler_params=_params(("arbitrary", "arbitrary")),
        name="conv_prompt",
    )(proj, proj, proj, buf, conv_w)


def _conv_step_kernel(gb_ref, gc_ref, v_ref, b0_ref, b1_ref, cw_ref, zprev_ref, z_ref, n0_ref, n1_ref):
    pre = gc_ref[...] * v_ref[...]
    b0 = b0_ref[...]
    b1 = b1_ref[...]
    cw = cw_ref[...]
    conv = cw[0:1, :] * b0 + cw[1:2, :] * b1 + cw[2:3, :] * pre
    z_ref[...] = (gb_ref[...] * conv).astype(z_ref.dtype)
    n0_ref[...] = b1
    n1_ref[...] = pre


def conv_step(proj, z_prev, row_start, n_seq, buf, conv_w):
    d = proj.shape[1] // 3
    tc = _pick(d, (512, 256, 128))
    ncb = d // tc
    rb = row_start // n_seq
    buf2 = buf.reshape(n_seq, 2 * d)
    row_spec = lambda off: pl.BlockSpec((n_seq, tc), lambda j: (rb, off * ncb + j))
    b_spec = lambda off: pl.BlockSpec((n_seq, tc), lambda j: (0, off * ncb + j))
    o_spec = pl.BlockSpec((n_seq, tc), lambda j: (0, j))
    z, n0, n1 = pl.pallas_call(
        _conv_step_kernel,
        grid=(ncb,),
        in_specs=[row_spec(0), row_spec(1), row_spec(2), b_spec(0), b_spec(1),
                  pl.BlockSpec((3, tc), lambda j: (0, j)),
                  pl.BlockSpec(memory_space=pl.ANY)],
        out_specs=[pl.BlockSpec((n_seq, tc), lambda j: (rb, j)), o_spec, o_spec],
        out_shape=[jax.ShapeDtypeStruct(z_prev.shape, BF16),
                   jax.ShapeDtypeStruct((n_seq, d), F32),
                   jax.ShapeDtypeStruct((n_seq, d), F32)],
        input_output_aliases={6: 0},
        compiler_params=_params(("arbitrary",)),
        name="conv_step",
    )(proj, proj, proj, buf2, buf2, conv_w, z_prev)
    return z, jnp.stack([n0, n1], axis=1)


def _hgrn_lower_bound(logit_ref, layer):
    lg = logit_ref[...]
    rows = [lg[i:i + 1, :] for i in range(lg.shape[0])]
    mx = functools.reduce(jnp.maximum, rows)
    es = [jnp.exp(r - mx) for r in rows]
    tot = functools.reduce(lambda a, b: a + b, es)
    part = functools.reduce(lambda a, b: a + b, es[1:layer + 1])
    return part / tot


def _hgrn_gate_out(o, gn, g):
    o = o * lax.rsqrt(jnp.mean(o * o, axis=-1, keepdims=True) + RMS_EPS)
    return o * gn * jax.nn.sigmoid(g)


def _hgrn_prompt_kernel(q_ref, f_ref, v_ref, g_ref, lgt_ref, gn_ref, s0_ref, z_ref, so_ref, st_ref,
                        *, layer, hb, tb, L):
    tblk = pl.program_id(2)
    lb = _hgrn_lower_bound(lgt_ref, layer)
    gn = gn_ref[...]

    @pl.when(tblk == 0)
    def _():
        for h in range(hb):
            st_ref[h] = s0_ref[0, h].T

    ri = lax.broadcasted_iota(jnp.int32, (L, L), 0)
    ci = lax.broadcasted_iota(jnp.int32, (L, L), 1)
    causal = ci <= ri
    tri = causal.astype(F32)
    mid = L // 2

    def chunk(c, carry):
        r0 = pl.multiple_of(c * L, L)
        fpre = f_ref[pl.ds(r0, L), :]
        f = lb + (1.0 - lb) * jax.nn.sigmoid(fpre)
        logf = jnp.log(f)
        kk = 1.0 - f
        b = jnp.dot(tri, logf, preferred_element_type=F32, precision=lax.Precision.HIGHEST)
        q = jax.nn.silu(q_ref[pl.ds(r0, L), :])
        v = v_ref[pl.ds(r0, L), :]
        g = g_ref[pl.ds(r0, L), :]
        b_end = b[L - 1:L, :]
        b_mid = b[mid - 1:mid, :]
        qt = (q * jnp.exp(b - b_mid)).astype(BF16)
        kt = (kk * jnp.exp(b_mid - b)).astype(BF16)
        qd = (q * jnp.exp(b)).astype(BF16)
        kd = (kk * jnp.exp(b_end - b)).astype(BF16)
        vb = v.astype(BF16)
        dec = jnp.exp(b_end)
        outs = []
        for h in range(hb):
            sl = slice(h * HG_DK, (h + 1) * HG_DK)
            att = lax.dot_general(qt[:, sl], kt[:, sl], (((1,), (1,)), ((), ())),
                                  preferred_element_type=F32)
            att = jnp.where(causal, att, 0.0).astype(BF16)
            s_t = st_ref[h]
            o = jnp.dot(att, vb[:, sl], preferred_element_type=F32)
            o = o + lax.dot_general(qd[:, sl], s_t.astype(BF16), (((1,), (1,)), ((), ())),
                                    preferred_element_type=F32)
            upd = lax.dot_general(vb[:, sl], kd[:, sl], (((0,), (0,)), ((), ())),
                                  preferred_element_type=F32)
            st_ref[h] = s_t * dec[:, sl] + upd
            outs.append(_hgrn_gate_out(o, gn[:, sl], g[:, sl]))
        z_ref[pl.ds(r0, L), :] = jnp.concatenate(outs, axis=1).astype(z_ref.dtype)
        return carry

    lax.fori_loop(0, tb // L, chunk, 0, unroll=2)

    @pl.when(tblk == pl.num_programs(2) - 1)
    def _():
        for h in range(hb):
            so_ref[0, h] = st_ref[h].T


def hgrn_prompt(proj, n_seq, t_len, s0, lb_logits, gnorm, layer):
    d = proj.shape[1] // 4
    heads = d // HG_DK
    hb = _pick(heads, (8, 4, 2, 1))
    nhb = heads // hb
    L = HG_CHUNK if t_len % HG_CHUNK == 0 else t_len
    tb = _pick(t_len, (512, 256, 128, 64))
    if tb % L:
        tb = t_len
    ntb = t_len // tb
    w = hb * HG_DK
    sec = lambda k: pl.BlockSpec((tb, w), lambda n, j, t: (n * ntb + t, k * nhb + j))
    kern = functools.partial(_hgrn_prompt_kernel, layer=layer, hb=hb, tb=tb, L=L)
    return pl.pallas_call(
        kern,
        grid=(n_seq, nhb, ntb),
        in_specs=[sec(0), sec(1), sec(2), sec(3),
                  pl.BlockSpec((lb_logits.shape[0], w), lambda n, j, t: (0, j)),
                  pl.BlockSpec((1, w), lambda n, j, t: (0, j)),
                  pl.BlockSpec((1, hb, HG_DK, HG_DK), lambda n, j, t: (n, j, 0, 0))],
        out_specs=[pl.BlockSpec((tb, w), lambda n, j, t: (n * ntb + t, j)),
                   pl.BlockSpec((1, hb, HG_DK, HG_DK), lambda n, j, t: (n, j, 0, 0))],
        out_shape=[jax.ShapeDtypeStruct((proj.shape[0], d), BF16),
                   jax.ShapeDtypeStruct(s0.shape, F32)],
        scratch_shapes=[pltpu.VMEM((hb, HG_DK, HG_DK), F32)],
        compiler_params=_params(("arbitrary", "arbitrary", "arbitrary")),
        name="hgrn_prompt",
    )(proj, proj, proj, proj, lb_logits, gnorm.reshape(1, d), s0)


def _hgrn_step_kernel(q_ref, f_ref, v_ref, g_ref, lgt_ref, gn_ref, s_ref, zprev_ref, z_ref, so_ref,
                      *, layer, nb):
    lb = _hgrn_lower_bound(lgt_ref, layer)
    f = lb + (1.0 - lb) * jax.nn.sigmoid(f_ref[...])
    q = jax.nn.silu(q_ref[...])
    v = v_ref[...]
    f_t = f.T
    k_t = 1.0 - f_t
    q_t = q.T
    outs = []
    for i in range(nb):
        s_new = f_t[:, i:i + 1] * s_ref[i, 0] + k_t[:, i:i + 1] * v[i:i + 1, :]
        so_ref[i, 0] = s_new
        outs.append(jnp.sum(q_t[:, i:i + 1] * s_new, axis=0, keepdims=True))
    o = jnp.concatenate(outs, axis=0)
    z_ref[...] = _hgrn_gate_out(o, gn_ref[...], g_ref[...]).astype(z_ref.dtype)


def hgrn_step(proj, z_prev, row_start, n_seq, s0, lb_logits, gnorm, layer):
    d = proj.shape[1] // 4
    heads = d // HG_DK
    nb = _pick(n_seq, (32, 16, 8))
    nnb = n_seq // nb
    rb = row_start // nb
    sec = lambda k: pl.BlockSpec((nb, HG_DK), lambda h, i: (rb + i, k * heads + h))
    s_spec = pl.BlockSpec((nb, 1, HG_DK, HG_DK), lambda h, i: (i, h, 0, 0))
    kern = functools.partial(_hgrn_step_kernel, layer=layer, nb=nb)
    return pl.pallas_call(
        kern,
        grid=(heads, nnb),
        in_specs=[sec(0), sec(1), sec(2), sec(3),
                  pl.BlockSpec((lb_logits.shape[0], HG_DK), lambda h, i: (0, h)),
                  pl.BlockSpec((1, HG_DK), lambda h, i: (0, h)),
                  s_spec,
                  pl.BlockSpec(memory_space=pl.ANY)],
        out_specs=[pl.BlockSpec((nb, HG_DK), lambda h, i: (rb + i, h)), s_spec],
        out_shape=[jax.ShapeDtypeStruct(z_prev.shape, BF16),
                   jax.ShapeDtypeStruct(s0.shape, F32)],
        input_output_aliases={7: 0},
        compiler_params=_params(("arbitrary", "arbitrary")),
        name="hgrn_step",
    )(proj, proj, proj, proj, lb_logits, gnorm.reshape(1, d), s0, z_prev)


def _row_tile(m, candidates):
    return _pick(m, candidates + (256, 128, 64, 32, 16))


def kernel(x_prompt, x_sample, state_s5_re, state_s5_im, state_conv, state_hgrn,
           norm_mix, norm_ffn, norm_final,
           s5_a_re, s5_a_im, s5_log_dt, s5_b_re, s5_b_im, s5_c_re, s5_c_im, s5_d, s5_w_glu,
           conv_w_in, conv_w, conv_w_out,
           hgrn_w_in, hgrn_lb_logits, hgrn_gnorm, hgrn_w_out,
           ffn_w_in, ffn_w_out):
    nb, t_len, d = x_prompt.shape
    ns = x_sample.shape[0]
    assert x_sample.shape[1] == 1
    mp = nb * t_len
    m = mp + ns
    depth = norm_mix.shape[0]
    g, p = s5_a_re.shape[1], s5_a_re.shape[2]
    gp = g * p
    dff = ffn_w_out.shape[1]
    heads = d // HG_DK
    tm_big = _row_tile(m, (1040,))
    tm_mid = _row_tile(m, (640,))
    tn_big = _pick(d, (512, 256, 128))
    tn_ff = _pick(dff, (256, 128))
    kk_ff = dff // 2 if (dff // 2) % LANES == 0 else dff

    h = jnp.concatenate([x_prompt.reshape(mp, d), x_sample.reshape(ns, d)], axis=0)
    hb = None
    z_state = jnp.zeros((nb, gp), F32)

    p_re, p_im, s_re, s_im, p_conv, s_conv, p_hg, s_hg = [], [], [], [], [], [], [], []
    for layer in range(depth):
        kind, j = layer % N_MIXERS, layer // N_MIXERS
        if kind == 0:
            u = rmsnorm_rows(h, norm_mix[layer], 0, m, F32, lane_groups=True)
            prm = (s5_a_re[j], s5_a_im[j], s5_log_dt[j], s5_b_re[j], s5_b_im[j],
                   s5_c_re[j], s5_c_im[j], s5_d[j])
            z, pre_, pim_ = s5_core(u, None, 0, nb, t_len, z_state, z_state, *prm)
            z, sre_, sim_ = s5_core(u, z, mp, ns, 1, state_s5_re[j].reshape(ns, gp),
                                    state_s5_im[j].reshape(ns, gp), *prm)
            p_re.append(pre_)
            p_im.append(pim_)
            s_re.append(sre_)
            s_im.append(sim_)
            tn = _pick(d, (256, 128))
            h, hb = dense(planes_to_rows(z, BF16), [(s5_w_glu, j, 0, 0), (s5_w_glu, j, 0, d // tn)], d,
                          epi=_epi_glu_res, out_dtype=F32, res=h, emit_bf16=True, tm=tm_big, tn=tn,
                          name="s5_glu")
        elif kind == 1:
            proj = dense(hb, [(conv_w_in, j, 0, 0)], 3 * d, epi=_epi_plain, out_dtype=F32,
                         norm_w=norm_mix[layer], tm=tm_big, tn=tn_big, name="conv_in")
            z, pb = conv_prompt(proj, nb, t_len, jnp.zeros((nb, 2, d), F32), conv_w[j])
            z, sb = conv_step(proj, z, mp, ns, state_conv[j], conv_w[j])
            p_conv.append(pb)
            s_conv.append(sb)
            h, hb = dense(z, [(conv_w_out, j, 0, 0)], d, epi=_epi_res, out_dtype=F32, res=h,
                          emit_bf16=True, tm=tm_big, tn=tn_big, name="conv_out")
        else:
            proj = dense(hb, [(hgrn_w_in, j, 0, 0)], 4 * d, epi=_epi_plain, out_dtype=F32,
                         norm_w=norm_mix[layer], tm=tm_big, tn=tn_big, name="hgrn_in")
            z, ps = hgrn_prompt(proj, nb, t_len, jnp.zeros((nb, heads, HG_DK, HG_DK), F32),
                                hgrn_lb_logits, hgrn_gnorm[j], layer)
            z, ss = hgrn_step(proj, z, mp, ns, state_hgrn[j], hgrn_lb_logits, hgrn_gnorm[j], layer)
            p_hg.append(ps)
            s_hg.append(ss)
            h, hb = dense(z, [(hgrn_w_out, j, 0, 0)], d, epi=_epi_res, out_dtype=F32, res=h,
                          emit_bf16=True, tm=tm_big, tn=tn_big, name="hgrn_out")
        hf = dense(hb, [(ffn_w_in, layer, 0, 0), (ffn_w_in, layer, 0, dff // tn_ff)], dff,
                   epi=_epi_swiglu, out_dtype=BF16, norm_w=norm_ffn[layer], tm=tm_big, tn=tn_ff,
                   name="ffn_in")
        n_kb = dff // kk_ff
        for kb in range(n_kb):
            last = kb == n_kb - 1
            out = dense(hf, [(ffn_w_out, layer, kb, 0)], d, epi=_epi_res, out_dtype=F32, res=h,
                        emit_bf16=last, tm=tm_mid, tn=tn_big, kk=kk_ff, x_kblock=kb, name="ffn_out")
            h, hb = out if last else (out, None)

    y_prompt = rmsnorm_rows(h, norm_final, 0, mp, F32).reshape(nb, t_len, d)
    y_sample = rmsnorm_rows(h, norm_final, mp, ns, F32).reshape(ns, 1, d)
    return (y_prompt, y_sample,
            jnp.stack(p_re).reshape(-1, nb, g, p), jnp.stack(p_im).reshape(-1, nb, g, p),
            jnp.stack(p_conv), jnp.stack(p_hg),
            jnp.stack(s_re).reshape(-1, ns, g, p), jnp.stack(s_im).reshape(-1, ns, g, p),
            jnp.stack(s_conv), jnp.stack(s_hg))
```

```python
import functools
import math

import jax
import jax.numpy as jnp
from jax import lax
from jax.experimental import pallas as pl
from jax.experimental.pallas import tpu as pltpu

F32 = jnp.float32
BF16 = jnp.bfloat16

RMS_EPS = 1e-6
N_MIXERS = 3
S5_GROUP = 16
S5_STATE = 64
S5_CHUNK = 8
HG_DK = 128
HG_CHUNK = 64
LANES = 128
S5_GPL = LANES // S5_GROUP
S5_HALF = S5_GPL * S5_STATE
VMEM_LIMIT = 56 * 1024 * 1024


def _params(sem):
    return pltpu.CompilerParams(dimension_semantics=sem, vmem_limit_bytes=VMEM_LIMIT)


def _pick(dim, candidates):
    for c in candidates:
        if dim % c == 0:
            return c
    return dim


def _rms(x, w):
    return x * lax.rsqrt(jnp.mean(x * x, axis=-1, keepdims=True) + RMS_EPS) * w


def _rmsnorm_kernel(x_ref, w_ref, o_ref, *, lane_groups):
    tm = x_ref.shape[0]
    rc = _pick(tm, (64, 32, 16, 8))
    for r in range(0, tm, rc):
        y = _rms(x_ref[r:r + rc, :], w_ref[...]).astype(o_ref.dtype)
        if lane_groups:
            for g in range(o_ref.shape[0]):
                o_ref[g, r:r + rc, :] = y[:, g * LANES:(g + 1) * LANES]
        else:
            o_ref[r:r + rc, :] = y


def rmsnorm_rows(x, w, row_start, n_rows, out_dtype, lane_groups=False):
    d = x.shape[1]
    tm = _pick(math.gcd(row_start, n_rows) if row_start else n_rows, (320, 256, 128, 64, 32, 16, 8))
    off = row_start // tm
    if lane_groups:
        out_spec = pl.BlockSpec((d // LANES, tm, LANES), lambda i: (0, i, 0))
        out_shape = jax.ShapeDtypeStruct((d // LANES, n_rows, LANES), out_dtype)
    else:
        out_spec = pl.BlockSpec((tm, d), lambda i: (i, 0))
        out_shape = jax.ShapeDtypeStruct((n_rows, d), out_dtype)
    return pl.pallas_call(
        functools.partial(_rmsnorm_kernel, lane_groups=lane_groups),
        grid=(n_rows // tm,),
        in_specs=[pl.BlockSpec((tm, d), lambda i: (i + off, 0)),
                  pl.BlockSpec((1, d), lambda i: (0, 0))],
        out_specs=out_spec,
        out_shape=out_shape,
        compiler_params=_params(("arbitrary",)),
        name="rmsnorm",
    )(x, w.reshape(1, d))


def _epi_plain(accs, res):
    return accs[0]


def _epi_res(accs, res):
    return res + accs[0]


def _epi_glu_res(accs, res):
    return res + accs[0] * jax.nn.sigmoid(accs[1])


def _epi_swiglu(accs, res):
    return jax.nn.silu(accs[0]) * accs[1]


def _dense_kernel(*refs, n_w, n_parts, n_tiles, n_pieces, has_res, has_norm, emit_bf16, epi):
    it = iter(refs)
    x_ref = next(it)
    nw_ref = next(it) if has_norm else None
    w_refs = [[next(it) for _ in range(n_parts)] for _ in range(n_w)]
    res_ref = next(it) if has_res else None
    o_ref = next(it)
    ob_ref = next(it) if emit_bf16 else None
    wb_refs = [next(it) for _ in range(n_w)]
    rinv_ref = next(it) if has_norm else None
    j = pl.program_id(0)
    i = pl.program_id(1)
    kp, tnp = w_refs[0][0].shape

    @pl.when(jnp.logical_and(j < n_tiles, i < n_pieces))
    def _cast():
        r0 = pl.multiple_of(i * kp, kp)
        for parts, wb_ref in zip(w_refs, wb_refs):
            for p, w_ref in enumerate(parts):
                w = w_ref[...]
                if has_norm:
                    w = w * jnp.concatenate([nw_ref[...]] * (tnp // LANES), axis=1)
                wb_ref[j % 2, pl.ds(r0, kp), p * tnp:(p + 1) * tnp] = w.astype(BF16)

    if has_norm:
        @pl.when(j == 1)
        def _row_scale():
            tm, k = x_ref.shape
            rc = _pick(tm, (208, 128, 64, 32, 16))
            for r in range(0, tm, rc):
                xf = x_ref[r:r + rc, :].astype(F32)
                ssq = jnp.sum(xf * xf, axis=-1, keepdims=True)
                rinv_ref[i, r:r + rc, :] = jnp.broadcast_to(lax.rsqrt(ssq / k + RMS_EPS), (rc, LANES))

    @pl.when(j > 0)
    def _compute():
        slot = (j + 1) % 2
        xb = x_ref[...]
        for p in range(n_parts):
            cols = slice(p * tnp, (p + 1) * tnp)
            accs = [jnp.dot(xb, wb[slot, :, cols], preferred_element_type=F32) for wb in wb_refs]
            if has_norm:
                rinv = rinv_ref[i]
                scale = jnp.concatenate([rinv] * (tnp // LANES), axis=1) if tnp > LANES else rinv
                accs = [a * scale for a in accs]
            res = res_ref[:, cols] if has_res else None
            out = epi(accs, res)
            o_ref[:, cols] = out.astype(o_ref.dtype)
            if emit_bf16:
                ob_ref[:, cols] = out.astype(BF16)


def dense(x, ws, n_out, *, epi, out_dtype, tm, tn, n_parts=1, res=None, norm_w=None, emit_bf16=False,
          kk=None, x_kblock=0, name="dense"):
    m = x.shape[0]
    kk = kk or x.shape[1]
    assert m % tm == 0 and tn % n_parts == 0
    assert norm_w is None or kk == x.shape[1]
    n_m = m // tm
    n_tiles = -(-n_out // tn)
    n_pieces = 8
    assert n_m >= n_pieces and kk % (16 * n_pieces) == 0
    kp = kk // n_pieces
    tnp = tn // n_parts
    piece = lambda j, i: jnp.where(j >= n_tiles, n_pieces - 1, jnp.minimum(i, n_pieces - 1))
    row_tile = lambda j, i: jnp.where(j == 0, 0, i)
    in_specs = [pl.BlockSpec((tm, kk), lambda j, i: (row_tile(j, i), x_kblock))]
    args = [x]
    if norm_w is not None:
        in_specs.append(pl.BlockSpec((kp, LANES), lambda j, i: (piece(j, i), 0)))
        args.append(jnp.broadcast_to(norm_w[:, None], (kk, LANES)))
    for w, lyr, rb, cb in ws:
        cb_max = w.shape[2] // tnp - 1
        for p in range(n_parts):
            in_specs.append(pl.BlockSpec(
                (None, kp, tnp),
                lambda j, i, lyr=lyr, rb=rb, cb=cb, p=p, cb_max=cb_max: (
                    lyr, rb * n_pieces + piece(j, i),
                    jnp.minimum(cb + jnp.minimum(j, n_tiles - 1) * n_parts + p, cb_max))))
            args.append(w)
    out_spec = pl.BlockSpec((tm, tn), lambda j, i: (row_tile(j, i), jnp.maximum(j - 1, 0)))
    if res is not None:
        in_specs.append(out_spec)
        args.append(res)
    out_specs = [out_spec]
    out_shape = [jax.ShapeDtypeStruct((m, n_out), out_dtype)]
    if emit_bf16:
        out_specs.append(out_spec)
        out_shape.append(jax.ShapeDtypeStruct((m, n_out), BF16))
    scratch = [pltpu.VMEM((2, kk, tn), BF16) for _ in ws]
    if norm_w is not None:
        scratch.append(pltpu.VMEM((n_m, tm, LANES), F32))
    kern = functools.partial(_dense_kernel, n_w=len(ws), n_parts=n_parts, n_tiles=n_tiles,
                             n_pieces=n_pieces, has_res=res is not None,
                             has_norm=norm_w is not None, emit_bf16=emit_bf16, epi=epi)
    outs = pl.pallas_call(
        kern,
        grid=(n_tiles + 1, n_m),
        in_specs=in_specs,
        out_specs=out_specs,
        out_shape=out_shape,
        scratch_shapes=scratch,
        compiler_params=_params(("arbitrary", "arbitrary")),
        name=name,
    )(*args)
    return outs if emit_bf16 else outs[0]


def _planes_to_rows_kernel(a_ref, b_ref, o_ref, *, n_a):
    def copy(src):
        for g in range(src.shape[0]):
            o_ref[:, g * LANES:(g + 1) * LANES] = src[g].astype(o_ref.dtype)

    @pl.when(pl.program_id(0) < n_a)
    def _():
        copy(a_ref)

    @pl.when(pl.program_id(0) >= n_a)
    def _():
        copy(b_ref)


def planes_to_rows(a_lg, b_lg, out_dtype):
    nlg, ma, _ = a_lg.shape
    mb = b_lg.shape[1]
    tm = _pick(math.gcd(ma, mb), (256, 128, 64, 32, 16))
    n_a, n_b = ma // tm, mb // tm
    return pl.pallas_call(
        functools.partial(_planes_to_rows_kernel, n_a=n_a),
        grid=(n_a + n_b,),
        in_specs=[pl.BlockSpec((nlg, tm, LANES), lambda i: (0, jnp.minimum(i, n_a - 1), 0)),
                  pl.BlockSpec((nlg, tm, LANES), lambda i: (0, jnp.maximum(i - n_a, 0), 0))],
        out_specs=pl.BlockSpec((tm, nlg * LANES), lambda i: (i, 0)),
        out_shape=jax.ShapeDtypeStruct((ma + mb, nlg * LANES), out_dtype),
        compiler_params=_params(("arbitrary",)),
        name="planes_to_rows",
    )(a_lg, b_lg)


def _s5_kernel(*refs, L, nc, n_seq):
    (u_ref, sre_ref, sim_ref, ar_ref, ai_ref, ldt_ref, br_ref, bi_ref, cr_ref, ci_ref, d_ref) = refs[:11]
    z_ref, ore_ref, oim_ref, tmat, qmat, pmat, binp, xst, xcat, xsb = refs[11:]
    H = S5_HALF

    ar = ar_ref[...]
    ai = ai_ref[...]
    dt = jnp.exp(ldt_ref[...])
    mag = jnp.exp(ar * dt)
    lr = mag * jnp.cos(ai * dt)
    li = mag * jnp.sin(ai * dt)
    den = ar * ar + ai * ai
    cor = ((lr - 1.0) * ar + li * ai) / den
    coi = (li * ar - (lr - 1.0) * ai) / den

    pows = [(jnp.ones_like(lr), jnp.zeros_like(lr))]
    for _ in range(L):
        pr, pi = pows[-1]
        pows.append((pr * lr - pi * li, pr * li + pi * lr))

    row_g = lax.broadcasted_iota(jnp.int32, (LANES, H), 0) // S5_GROUP
    lane_g = lax.broadcasted_iota(jnp.int32, (LANES, H), 1) // S5_STATE
    diag = row_g == lane_g
    br = jnp.where(diag, br_ref[...], 0.0)
    bi = jnp.where(diag, bi_ref[...], 0.0)
    cr = jnp.where(diag, cr_ref[...], 0.0)
    ci = jnp.where(diag, ci_ref[...], 0.0)
    a_r = br * cor - bi * coi
    a_i = br * coi + bi * cor
    a_cat = jnp.concatenate([a_r, a_i], axis=1).astype(BF16)
    zero_blk = jnp.zeros((LANES, LANES), BF16)
    for k in range(L + 1):
        pr, pi = pows[k]
        w_k = jnp.concatenate([cr * pr - ci * pi, -(cr * pi + ci * pr)], axis=1).astype(BF16)
        if k >= 1:
            pmat[(k - 1) * LANES:k * LANES, :] = w_k
        if k < L:
            d_k = lax.dot_general(a_cat, w_k, (((1,), (1,)), ((), ())),
                                  preferred_element_type=F32).astype(BF16)
            for s in range(L - k):
                t = s + k
                tmat[s * LANES:(s + 1) * LANES, t * LANES:(t + 1) * LANES] = d_k
            s = L - 1 - k
            qmat[s * LANES:(s + 1) * LANES, :] = jnp.concatenate(
                [a_r * pr - a_i * pi, a_r * pi + a_i * pr], axis=1).astype(BF16)
    for s in range(L):
        for t in range(s):
            tmat[s * LANES:(s + 1) * LANES, t * LANES:(t + 1) * LANES] = zero_blk

    hp = H // LANES
    lanes = lambda a, k: a[:, k * LANES:(k + 1) * LANES]
    ct = 2 * LANES if L % 2 == 0 else LANES

    n_rows = n_seq * nc
    pos = lambda s: pl.ds(s, n_rows, stride=L) if L > 1 else slice(None)

    for s in range(L):
        xcat[:, s * LANES:(s + 1) * LANES] = u_ref[pos(s), :].astype(BF16)

    slots = binp.shape[1] // nc
    seq_rows = lambda n: pl.ds(n, nc, stride=slots)
    for k0 in range(0, 2 * hp, 2):
        b2 = jnp.dot(xcat[...], qmat[:, k0 * LANES:(k0 + 2) * LANES], preferred_element_type=F32)
        if nc == 1:
            binp[k0] = b2[:, :LANES]
            binp[k0 + 1] = b2[:, LANES:]
        else:
            for n in range(n_seq):
                binp[k0, seq_rows(n), :] = b2[n * nc:(n + 1) * nc, :LANES]
                binp[k0 + 1, seq_rows(n), :] = b2[n * nc:(n + 1) * nc, LANES:]
    lLr, lLi = pows[L]
    lam_r = [lanes(lLr, k) for k in range(hp)]
    lam_i = [lanes(lLi, k) for k in range(hp)]
    x0r = sre_ref[...]
    x0i = sim_ref[...]

    if nc == 1:
        for k in range(hp):
            xr, xi = lanes(x0r, k), lanes(x0i, k)
            xsb[:, k * LANES:(k + 1) * LANES] = xr.astype(BF16)
            xsb[:, (hp + k) * LANES:(hp + k + 1) * LANES] = xi.astype(BF16)
            ore_ref[:, k * LANES:(k + 1) * LANES] = xr * lam_r[k] - xi * lam_i[k] + binp[k]
            oim_ref[:, k * LANES:(k + 1) * LANES] = xr * lam_i[k] + xi * lam_r[k] + binp[hp + k]
    else:
        def step(c, carry):
            rows = pl.ds(pl.multiple_of(c * slots, slots), n_seq)
            new = []
            for k in range(hp):
                xr, xi = carry[k], carry[hp + k]
                xst[k, rows, :] = xr
                xst[hp + k, rows, :] = xi
                new.append((xr * lam_r[k] - xi * lam_i[k] + binp[k, rows, :],
                            xr * lam_i[k] + xi * lam_r[k] + binp[hp + k, rows, :]))
            return tuple(r for r, _ in new) + tuple(i for _, i in new)

        init = tuple(lanes(x0r, k) for k in range(hp)) + tuple(lanes(x0i, k) for k in range(hp))
        fin = lax.fori_loop(0, nc, step, init, unroll=4)
        ore_ref[...] = jnp.concatenate(fin[:hp], axis=1)
        oim_ref[...] = jnp.concatenate(fin[hp:], axis=1)
        for k in range(2 * hp):
            for n in range(n_seq):
                xsb[n * nc:(n + 1) * nc, k * LANES:(k + 1) * LANES] = xst[k, seq_rows(n), :].astype(BF16)

    dsk = d_ref[...]
    for c0 in range(0, L * LANES, ct):
        kmax = c0 + ct
        y = jnp.dot(xcat[:, :kmax], tmat[:kmax, c0:c0 + ct], preferred_element_type=F32)
        y = y + lax.dot_general(xsb[...], pmat[c0:c0 + ct, :], (((1,), (1,)), ((), ())),
                                preferred_element_type=F32)
        for t in range(c0 // LANES, (c0 + ct) // LANES):
            yt = y[:, t * LANES - c0:(t + 1) * LANES - c0] + dsk * u_ref[pos(t), :]
            z_ref[pos(t), :] = jax.nn.gelu(yt)


def _s5_lane_rows(a):
    g, p = a.shape
    return a.reshape(g // S5_GPL, 1, S5_GPL * p)


def _s5_tiles(a):
    g, c, p = a.shape
    return jnp.tile(a.reshape(g // S5_GPL, S5_GPL * c, p), (1, 1, S5_GPL))


def s5_core(u_lg, row_start, n_seq, t_len, st_re, st_im,
            a_re, a_im, log_dt, b_re, b_im, c_re, c_im, d_skip):
    nlg, m, _ = u_lg.shape
    g, p = a_re.shape
    H = S5_HALF
    L = S5_CHUNK if t_len % S5_CHUNK == 0 else 1
    assert L > 1 or t_len == 1
    nc = t_len // L
    rows = n_seq * nc
    assert row_start % (rows * L) == 0 and m % L == 0
    rb = row_start // (rows * L)
    lane_args = [_s5_lane_rows(a_re), _s5_lane_rows(a_im),
                 _s5_lane_rows(jnp.broadcast_to(log_dt[:, None], (g, p)))]
    tile_args = [_s5_tiles(jnp.swapaxes(b_re, 1, 2)), _s5_tiles(jnp.swapaxes(b_im, 1, 2)),
                 _s5_tiles(c_re), _s5_tiles(c_im)]
    d_arg = d_skip.reshape(nlg, 1, LANES)
    lane_spec = pl.BlockSpec((None, 1, H), lambda j: (j, 0, 0))
    tile_spec = pl.BlockSpec((None, LANES, H), lambda j: (j, 0, 0))
    d_spec = pl.BlockSpec((None, 1, LANES), lambda j: (j, 0, 0))
    sublanes = 8
    plane_rows = rows if nc == 1 else nc * (-(-n_seq // sublanes) * sublanes)
    u_spec = pl.BlockSpec((None, rows * L, LANES), lambda j: (j, rb, 0))
    z_spec = pl.BlockSpec((None, rows * L, LANES), lambda j: (j, 0, 0))
    st_spec = pl.BlockSpec((n_seq, H), lambda j: (0, j))
    st_shape = jax.ShapeDtypeStruct((n_seq, g * p), F32)
    in_specs = [u_spec, st_spec, st_spec] + [lane_spec] * 3 + [tile_spec] * 4 + [d_spec]
    args = [u_lg, st_re, st_im, *lane_args, *tile_args, d_arg]
    kern = functools.partial(_s5_kernel, L=L, nc=nc, n_seq=n_seq)
    z, o_re, o_im = pl.pallas_call(
        kern,
        grid=(nlg,),
        in_specs=in_specs,
        out_specs=[z_spec, st_spec, st_spec],
        out_shape=[jax.ShapeDtypeStruct((nlg, rows * L, LANES), F32), st_shape, st_shape],
        scratch_shapes=[pltpu.VMEM((L * LANES, L * LANES), BF16),
                        pltpu.VMEM((L * LANES, 2 * H), BF16),
                        pltpu.VMEM((L * LANES, 2 * H), BF16),
                        pltpu.VMEM((2 * H // LANES, plane_rows, LANES), F32),
                        pltpu.VMEM((2 * H // LANES, plane_rows, LANES), F32),
                        pltpu.VMEM((rows, L * LANES), BF16),
                        pltpu.VMEM((rows, 2 * H), BF16)],
        compiler_params=_params(("arbitrary",)),
        name="s5_core",
    )(*args)
    return z, o_re, o_im


def _conv_prompt_kernel(gb_ref, gc_ref, v_ref, buf_ref, cw_ref, zprev_ref, z_ref, nb_ref):
    pre = gc_ref[...] * v_ref[...]
    t = pre.shape[0]
    buf = buf_ref[...]
    row = lax.broadcasted_iota(jnp.int32, pre.shape, 0)
    sh1 = jnp.where(row == 0, buf[1:2, :], pltpu.roll(pre, 1, 0))
    sh2 = pltpu.roll(pre, 2, 0)
    sh2 = jnp.where(row == 0, buf[0:1, :], jnp.where(row == 1, buf[1:2, :], sh2))
    cw = cw_ref[...]
    conv = cw[0:1, :] * sh2 + cw[1:2, :] * sh1 + cw[2:3, :] * pre
    z_ref[...] = (gb_ref[...] * conv).astype(z_ref.dtype)
    nb_ref[...] = pre[t - 2:t, :]


def conv_prompt(proj, z_prev, n_seq, t_len, buf, conv_w):
    d = proj.shape[1] // 3
    tc = _pick(d, (256, 128))
    ncb = d // tc
    return pl.pallas_call(
        _conv_prompt_kernel,
        grid=(n_seq, ncb),
        in_specs=[pl.BlockSpec((t_len, tc), lambda n, j: (n, j)),
                  pl.BlockSpec((t_len, tc), lambda n, j: (n, ncb + j)),
                  pl.BlockSpec((t_len, tc), lambda n, j: (n, 2 * ncb + j)),
                  pl.BlockSpec((None, 2, tc), lambda n, j: (n, 0, j)),
                  pl.BlockSpec((3, tc), lambda n, j: (0, j)),
                  pl.BlockSpec(memory_space=pl.ANY)],
        out_specs=[pl.BlockSpec((t_len, tc), lambda n, j: (n, j)),
                   pl.BlockSpec((None, 2, tc), lambda n, j: (n, 0, j))],
        out_shape=[jax.ShapeDtypeStruct(z_prev.shape, BF16),
                   jax.ShapeDtypeStruct((n_seq, 2, d), F32)],
        input_output_aliases={5: 0},
        compiler_params=_params(("arbitrary", "arbitrary")),
        name="conv_prompt",
    )(proj, proj, proj, buf, conv_w, z_prev)


def _conv_step_kernel(gb_ref, gc_ref, v_ref, b0_ref, b1_ref, cw_ref, zprev_ref, z_ref, n0_ref, n1_ref):
    pre = gc_ref[...] * v_ref[...]
    b0 = b0_ref[...]
    b1 = b1_ref[...]
    cw = cw_ref[...]
    conv = cw[0:1, :] * b0 + cw[1:2, :] * b1 + cw[2:3, :] * pre
    z_ref[...] = (gb_ref[...] * conv).astype(z_ref.dtype)
    n0_ref[...] = b1
    n1_ref[...] = pre


def conv_step(proj, z_prev, row_start, n_seq, buf, conv_w):
    d = proj.shape[1] // 3
    tc = _pick(d, (512, 256, 128))
    ncb = d // tc
    rb = row_start // n_seq
    buf2 = buf.reshape(n_seq, 2 * d)
    row_spec = lambda off: pl.BlockSpec((n_seq, tc), lambda j: (rb, off * ncb + j))
    b_spec = lambda off: pl.BlockSpec((n_seq, tc), lambda j: (0, off * ncb + j))
    o_spec = pl.BlockSpec((n_seq, tc), lambda j: (0, j))
    z, n0, n1 = pl.pallas_call(
        _conv_step_kernel,
        grid=(ncb,),
        in_specs=[row_spec(0), row_spec(1), row_spec(2), b_spec(0), b_spec(1),
                  pl.BlockSpec((3, tc), lambda j: (0, j)),
                  pl.BlockSpec(memory_space=pl.ANY)],
        out_specs=[pl.BlockSpec((n_seq, tc), lambda j: (rb, j)), o_spec, o_spec],
        out_shape=[jax.ShapeDtypeStruct(z_prev.shape, BF16),
                   jax.ShapeDtypeStruct((n_seq, d), F32),
                   jax.ShapeDtypeStruct((n_seq, d), F32)],
        input_output_aliases={6: 0},
        compiler_params=_params(("arbitrary",)),
        name="conv_step",
    )(proj, proj, proj, buf2, buf2, conv_w, z_prev)
    return z, jnp.stack([n0, n1], axis=1)


def _hgrn_lower_bound(logit_ref, layer):
    lg = logit_ref[...]
    rows = [lg[i:i + 1, :] for i in range(lg.shape[0])]
    mx = functools.reduce(jnp.maximum, rows)
    es = [jnp.exp(r - mx) for r in rows]
    tot = functools.reduce(lambda a, b: a + b, es)
    part = functools.reduce(lambda a, b: a + b, es[1:layer + 1])
    return part / tot


def _hgrn_gate_out(o, gn, g):
    o = o * lax.rsqrt(jnp.mean(o * o, axis=-1, keepdims=True) + RMS_EPS)
    return o * gn * jax.nn.sigmoid(g)


def _hgrn_prompt_kernel(q_ref, f_ref, v_ref, g_ref, lgt_ref, gn_ref, s0_ref, zprev_ref, z_ref, so_ref,
                        st_ref, *, layer, hb, tb, L):
    tblk = pl.program_id(2)
    lb = _hgrn_lower_bound(lgt_ref, layer)
    gn = gn_ref[...]

    @pl.when(tblk == 0)
    def _():
        for h in range(hb):
            st_ref[h] = s0_ref[0, h].T

    ri = lax.broadcasted_iota(jnp.int32, (L, L), 0)
    ci = lax.broadcasted_iota(jnp.int32, (L, L), 1)
    causal = ci <= ri
    tri = causal.astype(F32)
    mid = L // 2

    def chunk(c, carry):
        r0 = pl.multiple_of(c * L, L)
        fpre = f_ref[pl.ds(r0, L), :]
        f = lb + (1.0 - lb) * jax.nn.sigmoid(fpre)
        logf = jnp.log(f)
        kk = 1.0 - f
        b = jnp.dot(tri, logf, preferred_element_type=F32, precision=lax.Precision.HIGHEST)
        q = jax.nn.silu(q_ref[pl.ds(r0, L), :])
        v = v_ref[pl.ds(r0, L), :]
        g = g_ref[pl.ds(r0, L), :]
        b_end = b[L - 1:L, :]
        b_mid = b[mid - 1:mid, :]
        qt = (q * jnp.exp(b - b_mid)).astype(BF16)
        kt = (kk * jnp.exp(b_mid - b)).astype(BF16)
        qd = (q * jnp.exp(b)).astype(BF16)
        kd = (kk * jnp.exp(b_end - b)).astype(BF16)
        vb = v.astype(BF16)
        dec = jnp.exp(b_end)
        outs = []
        for h in range(hb):
            sl = slice(h * HG_DK, (h + 1) * HG_DK)
            att = lax.dot_general(qt[:, sl], kt[:, sl], (((1,), (1,)), ((), ())),
                                  preferred_element_type=F32)
            att = jnp.where(causal, att, 0.0).astype(BF16)
            s_t = st_ref[h]
            o = jnp.dot(att, vb[:, sl], preferred_element_type=F32)
            o = o + lax.dot_general(qd[:, sl], s_t.astype(BF16), (((1,), (1,)), ((), ())),
                                    preferred_element_type=F32)
            upd = lax.dot_general(vb[:, sl], kd[:, sl], (((0,), (0,)), ((), ())),
                                  preferred_element_type=F32)
            st_ref[h] = s_t * dec[:, sl] + upd
            outs.append(_hgrn_gate_out(o, gn[:, sl], g[:, sl]))
        z_ref[pl.ds(r0, L), :] = jnp.concatenate(outs, axis=1).astype(z_ref.dtype)
        return carry

    lax.fori_loop(0, tb // L, chunk, 0, unroll=2)

    @pl.when(tblk == pl.num_programs(2) - 1)
    def _():
        for h in range(hb):
            so_ref[0, h] = st_ref[h].T


def hgrn_prompt(proj, z_prev, n_seq, t_len, s0, lb_logits, gnorm, layer):
    d = proj.shape[1] // 4
    heads = d // HG_DK
    hb = _pick(heads, (8, 4, 2, 1))
    nhb = heads // hb
    L = HG_CHUNK if t_len % HG_CHUNK == 0 else t_len
    tb = _pick(t_len, (512, 256, 128, 64))
    if tb % L:
        tb = t_len
    ntb = t_len // tb
    w = hb * HG_DK
    sec = lambda k: pl.BlockSpec((tb, w), lambda n, j, t: (n * ntb + t, k * nhb + j))
    kern = functools.partial(_hgrn_prompt_kernel, layer=layer, hb=hb, tb=tb, L=L)
    return pl.pallas_call(
        kern,
        grid=(n_seq, nhb, ntb),
        in_specs=[sec(0), sec(1), sec(2), sec(3),
                  pl.BlockSpec((lb_logits.shape[0], w), lambda n, j, t: (0, j)),
                  pl.BlockSpec((1, w), lambda n, j, t: (0, j)),
                  pl.BlockSpec((1, hb, HG_DK, HG_DK), lambda n, j, t: (n, j, 0, 0)),
                  pl.BlockSpec(memory_space=pl.ANY)],
        out_specs=[pl.BlockSpec((tb, w), lambda n, j, t: (n * ntb + t, j)),
                   pl.BlockSpec((1, hb, HG_DK, HG_DK), lambda n, j, t: (n, j, 0, 0))],
        out_shape=[jax.ShapeDtypeStruct(z_prev.shape, BF16),
                   jax.ShapeDtypeStruct(s0.shape, F32)],
        scratch_shapes=[pltpu.VMEM((hb, HG_DK, HG_DK), F32)],
        input_output_aliases={7: 0},
        compiler_params=_params(("arbitrary", "arbitrary", "arbitrary")),
        name="hgrn_prompt",
    )(proj, proj, proj, proj, lb_logits, gnorm.reshape(1, d), s0, z_prev)


def _hgrn_step_kernel(q_ref, f_ref, v_ref, g_ref, lgt_ref, gn_ref, s_ref, zprev_ref, z_ref, so_ref,
                      *, layer, nb):
    lb = _hgrn_lower_bound(lgt_ref, layer)
    f = lb + (1.0 - lb) * jax.nn.sigmoid(f_ref[...])
    q = jax.nn.silu(q_ref[...])
    v = v_ref[...]
    f_t = f.T
    k_t = 1.0 - f_t
    q_t = q.T
    outs = []
    for i in range(nb):
        s_new = f_t[:, i:i + 1] * s_ref[i, 0] + k_t[:, i:i + 1] * v[i:i + 1, :]
        so_ref[i, 0] = s_new
        outs.append(jnp.sum(q_t[:, i:i + 1] * s_new, axis=0, keepdims=True))
    o = jnp.concatenate(outs, axis=0)
    z_ref[...] = _hgrn_gate_out(o, gn_ref[...], g_ref[...]).astype(z_ref.dtype)


def hgrn_step(proj, z_prev, row_start, n_seq, s0, lb_logits, gnorm, layer):
    d = proj.shape[1] // 4
    heads = d // HG_DK
    nb = _pick(n_seq, (32, 16, 8))
    nnb = n_seq // nb
    rb = row_start // nb
    sec = lambda k: pl.BlockSpec((nb, HG_DK), lambda h, i: (rb + i, k * heads + h))
    s_spec = pl.BlockSpec((nb, 1, HG_DK, HG_DK), lambda h, i: (i, h, 0, 0))
    kern = functools.partial(_hgrn_step_kernel, layer=layer, nb=nb)
    return pl.pallas_call(
        kern,
        grid=(heads, nnb),
        in_specs=[sec(0), sec(1), sec(2), sec(3),
                  pl.BlockSpec((lb_logits.shape[0], HG_DK), lambda h, i: (0, h)),
                  pl.BlockSpec((1, HG_DK), lambda h, i: (0, h)),
                  s_spec,
                  pl.BlockSpec(memory_space=pl.ANY)],
        out_specs=[pl.BlockSpec((nb, HG_DK), lambda h, i: (rb + i, h)), s_spec],
        out_shape=[jax.ShapeDtypeStruct(z_prev.shape, BF16),
                   jax.ShapeDtypeStruct(s0.shape, F32)],
        input_output_aliases={7: 0},
        compiler_params=_params(("arbitrary", "arbitrary")),
        name="hgrn_step",
    )(proj, proj, proj, proj, lb_logits, gnorm.reshape(1, d), s0, z_prev)


def _row_tile(m, candidates):
    return _pick(m, candidates + (256, 128, 64, 32, 16))


def kernel(x_prompt, x_sample, state_s5_re, state_s5_im, state_conv, state_hgrn,
           norm_mix, norm_ffn, norm_final,
           s5_a_re, s5_a_im, s5_log_dt, s5_b_re, s5_b_im, s5_c_re, s5_c_im, s5_d, s5_w_glu,
           conv_w_in, conv_w, conv_w_out,
           hgrn_w_in, hgrn_lb_logits, hgrn_gnorm, hgrn_w_out,
           ffn_w_in, ffn_w_out):
    nb, t_len, d = x_prompt.shape
    ns = x_sample.shape[0]
    assert x_sample.shape[1] == 1
    mp = nb * t_len
    m = mp + ns
    depth = norm_mix.shape[0]
    g, p = s5_a_re.shape[1], s5_a_re.shape[2]
    gp = g * p
    dff = ffn_w_out.shape[1]
    heads = d // HG_DK
    tm_big = _row_tile(m, (1040,))
    tm_ffn = _row_tile(m, (832,))
    tn_big = _pick(d, (512, 256, 128))
    tnp_ff = _pick(dff, (256, 128))
    kk_ff = dff // 2 if (dff // 2) % LANES == 0 else dff

    h = jnp.concatenate([x_prompt.reshape(mp, d), x_sample.reshape(ns, d)], axis=0)
    hb = None
    z_state = jnp.zeros((nb, gp), F32)

    p_re, p_im, s_re, s_im, p_conv, s_conv, p_hg, s_hg = [], [], [], [], [], [], [], []
    for layer in range(depth):
        kind, j = layer % N_MIXERS, layer // N_MIXERS
        if kind == 0:
            u = rmsnorm_rows(h, norm_mix[layer], 0, m, F32, lane_groups=True)
            prm = (s5_a_re[j], s5_a_im[j], s5_log_dt[j], s5_b_re[j], s5_b_im[j],
                   s5_c_re[j], s5_c_im[j], s5_d[j])
            zp, pre_, pim_ = s5_core(u, 0, nb, t_len, z_state, z_state, *prm)
            zs, sre_, sim_ = s5_core(u, mp, ns, 1, state_s5_re[j].reshape(ns, gp),
                                     state_s5_im[j].reshape(ns, gp), *prm)
            p_re.append(pre_)
            p_im.append(pim_)
            s_re.append(sre_)
            s_im.append(sim_)
            tn = _pick(d, (256, 128))
            h, hb = dense(planes_to_rows(zp, zs, BF16), [(s5_w_glu, j, 0, 0), (s5_w_glu, j, 0, d // tn)], d,
                          epi=_epi_glu_res, out_dtype=F32, res=h, emit_bf16=True, tm=tm_big, tn=tn,
                          name="s5_glu")
        elif kind == 1:
            proj = dense(hb, [(conv_w_in, j, 0, 0)], 3 * d, epi=_epi_plain, out_dtype=F32,
                         norm_w=norm_mix[layer], tm=tm_big, tn=tn_big, name="conv_in")
            z, pb = conv_prompt(proj, jnp.zeros((m, d), BF16), nb, t_len, jnp.zeros((nb, 2, d), F32),
                                conv_w[j])
            z, sb = conv_step(proj, z, mp, ns, state_conv[j], conv_w[j])
            p_conv.append(pb)
            s_conv.append(sb)
            h, hb = dense(z, [(conv_w_out, j, 0, 0)], d, epi=_epi_res, out_dtype=F32, res=h,
                          emit_bf16=True, tm=tm_big, tn=tn_big, name="conv_out")
        else:
            proj = dense(hb, [(hgrn_w_in, j, 0, 0)], 4 * d, epi=_epi_plain, out_dtype=F32,
                         norm_w=norm_mix[layer], tm=tm_big, tn=tn_big, name="hgrn_in")
            z, ps = hgrn_prompt(proj, jnp.zeros((m, d), BF16), nb, t_len,
                                jnp.zeros((nb, heads, HG_DK, HG_DK), F32),
                                hgrn_lb_logits, hgrn_gnorm[j], layer)
            z, ss = hgrn_step(proj, z, mp, ns, state_hgrn[j], hgrn_lb_logits, hgrn_gnorm[j], layer)
            p_hg.append(ps)
            s_hg.append(ss)
            h, hb = dense(z, [(hgrn_w_out, j, 0, 0)], d, epi=_epi_res, out_dtype=F32, res=h,
                          emit_bf16=True, tm=tm_big, tn=tn_big, name="hgrn_out")
        hf = dense(hb, [(ffn_w_in, layer, 0, 0), (ffn_w_in, layer, 0, dff // tnp_ff)], dff,
                   epi=_epi_swiglu, out_dtype=BF16, norm_w=norm_ffn[layer], tm=tm_ffn,
                   tn=2 * tnp_ff, n_parts=2, name="ffn_in")
        n_kb = dff // kk_ff
        for kb in range(n_kb):
            last = kb == n_kb - 1
            out = dense(hf, [(ffn_w_out, layer, kb, 0)], d, epi=_epi_res, out_dtype=F32, res=h,
                        emit_bf16=last, tm=tm_big, tn=tn_big, kk=kk_ff, x_kblock=kb, name="ffn_out")
            h, hb = out if last else (out, None)

    y_prompt = rmsnorm_rows(h, norm_final, 0, mp, F32).reshape(nb, t_len, d)
    y_sample = rmsnorm_rows(h, norm_final, mp, ns, F32).reshape(ns, 1, d)
    return (y_prompt, y_sample,
            jnp.stack(p_re).reshape(-1, nb, g, p), jnp.stack(p_im).reshape(-1, nb, g, p),
            jnp.stack(p_conv), jnp.stack(p_hg),
            jnp.stack(s_re).reshape(-1, ns, g, p), jnp.stack(s_im).reshape(-1, ns, g, p),
            jnp.stack(s_conv), jnp.stack(s_hg))
```

```python
import functools
import math

import jax
import jax.numpy as jnp
from jax import lax
from jax.experimental import pallas as pl
from jax.experimental.pallas import tpu as pltpu

F32 = jnp.float32
BF16 = jnp.bfloat16

RMS_EPS = 1e-6
N_MIXERS = 3
S5_GROUP = 16
S5_STATE = 64
S5_CHUNK = 8
HG_DK = 128
HG_CHUNK = 64
LANES = 128
S5_GPL = LANES // S5_GROUP
S5_HALF = S5_GPL * S5_STATE
VMEM_LIMIT = 60 * 1024 * 1024


def _params(sem):
    return pltpu.CompilerParams(dimension_semantics=sem, vmem_limit_bytes=VMEM_LIMIT)


def _pick(dim, candidates):
    for c in candidates:
        if dim % c == 0:
            return c
    return dim


def _rms(x, w):
    return x * lax.rsqrt(jnp.mean(x * x, axis=-1, keepdims=True) + RMS_EPS) * w


def _rmsnorm_kernel(x_ref, w_ref, o_ref, *, lane_groups):
    tm = x_ref.shape[0]
    rc = _pick(tm, (64, 32, 16, 8))
    for r in range(0, tm, rc):
        y = _rms(x_ref[r:r + rc, :], w_ref[...]).astype(o_ref.dtype)
        if lane_groups:
            for g in range(o_ref.shape[0]):
                o_ref[g, r:r + rc, :] = y[:, g * LANES:(g + 1) * LANES]
        else:
            o_ref[r:r + rc, :] = y


def rmsnorm_rows(x, w, row_start, n_rows, out_dtype, lane_groups=False):
    d = x.shape[1]
    tm = _pick(math.gcd(row_start, n_rows) if row_start else n_rows, (320, 256, 128, 64, 32, 16, 8))
    off = row_start // tm
    if lane_groups:
        out_spec = pl.BlockSpec((d // LANES, tm, LANES), lambda i: (0, i, 0))
        out_shape = jax.ShapeDtypeStruct((d // LANES, n_rows, LANES), out_dtype)
    else:
        out_spec = pl.BlockSpec((tm, d), lambda i: (i, 0))
        out_shape = jax.ShapeDtypeStruct((n_rows, d), out_dtype)
    return pl.pallas_call(
        functools.partial(_rmsnorm_kernel, lane_groups=lane_groups),
        grid=(n_rows // tm,),
        in_specs=[pl.BlockSpec((tm, d), lambda i: (i + off, 0)),
                  pl.BlockSpec((1, d), lambda i: (0, 0))],
        out_specs=out_spec,
        out_shape=out_shape,
        compiler_params=_params(("arbitrary",)),
        name="rmsnorm",
    )(x, w.reshape(1, d))


def _epi_plain(accs, res):
    return accs[0]


def _epi_res(accs, res):
    return res + accs[0]


def _epi_glu_res(accs, res):
    return res + accs[0] * jax.nn.sigmoid(accs[1])


def _epi_swiglu(accs, res):
    return jax.nn.silu(accs[0]) * accs[1]


def _dense_kernel(*refs, n_w, n_parts, n_tiles, n_pieces, has_res, has_norm, emit_bf16, epi):
    it = iter(refs)
    x_ref = next(it)
    nw_ref = next(it) if has_norm else None
    w_refs = [[next(it) for _ in range(n_parts)] for _ in range(n_w)]
    res_ref = next(it) if has_res else None
    o_ref = next(it)
    ob_ref = next(it) if emit_bf16 else None
    wb_refs = [next(it) for _ in range(n_w)]
    rinv_ref = next(it) if has_norm else None
    j = pl.program_id(0)
    i = pl.program_id(1)
    kp, tnp = w_refs[0][0].shape

    @pl.when(jnp.logical_and(j < n_tiles, i < n_pieces))
    def _cast():
        r0 = pl.multiple_of(i * kp, kp)
        for parts, wb_ref in zip(w_refs, wb_refs):
            for p, w_ref in enumerate(parts):
                w = w_ref[...]
                if has_norm:
                    w = w * jnp.concatenate([nw_ref[...]] * (tnp // LANES), axis=1)
                wb_ref[j % 2, pl.ds(r0, kp), p * tnp:(p + 1) * tnp] = w.astype(BF16)

    if has_norm:
        @pl.when(j == 1)
        def _row_scale():
            tm, k = x_ref.shape
            rc = _pick(tm, (208, 128, 64, 32, 16))
            for r in range(0, tm, rc):
                xf = x_ref[r:r + rc, :].astype(F32)
                ssq = jnp.sum(xf * xf, axis=-1, keepdims=True)
                rinv_ref[i, r:r + rc, :] = jnp.broadcast_to(lax.rsqrt(ssq / k + RMS_EPS), (rc, LANES))

    @pl.when(j > 0)
    def _compute():
        slot = (j + 1) % 2
        xb = x_ref[...]
        for p in range(n_parts):
            cols = slice(p * tnp, (p + 1) * tnp)
            accs = [jnp.dot(xb, wb[slot, :, cols], preferred_element_type=F32) for wb in wb_refs]
            if has_norm:
                rinv = rinv_ref[i]
                scale = jnp.concatenate([rinv] * (tnp // LANES), axis=1) if tnp > LANES else rinv
                accs = [a * scale for a in accs]
            res = res_ref[:, cols] if has_res else None
            out = epi(accs, res)
            o_ref[:, cols] = out.astype(o_ref.dtype)
            if emit_bf16:
                ob_ref[:, cols] = out.astype(BF16)


def dense(x, ws, n_out, *, epi, out_dtype, tm, tn, n_parts=1, res=None, norm_w=None, emit_bf16=False,
          kk=None, x_kblock=0, name="dense"):
    m = x.shape[0]
    kk = kk or x.shape[1]
    assert m % tm == 0 and tn % n_parts == 0
    assert norm_w is None or kk == x.shape[1]
    n_m = m // tm
    n_tiles = -(-n_out // tn)
    n_pieces = 8
    assert n_m >= n_pieces and kk % (16 * n_pieces) == 0
    kp = kk // n_pieces
    tnp = tn // n_parts
    piece = lambda j, i: jnp.where(j >= n_tiles, n_pieces - 1, jnp.minimum(i, n_pieces - 1))
    row_tile = lambda j, i: jnp.where(j == 0, 0, i)
    in_specs = [pl.BlockSpec((tm, kk), lambda j, i: (row_tile(j, i), x_kblock))]
    args = [x]
    if norm_w is not None:
        in_specs.append(pl.BlockSpec((kp, LANES), lambda j, i: (piece(j, i), 0)))
        args.append(jnp.broadcast_to(norm_w[:, None], (kk, LANES)))
    for w, lyr, rb, cb in ws:
        cb_max = w.shape[2] // tnp - 1
        for p in range(n_parts):
            in_specs.append(pl.BlockSpec(
                (None, kp, tnp),
                lambda j, i, lyr=lyr, rb=rb, cb=cb, p=p, cb_max=cb_max: (
                    lyr, rb * n_pieces + piece(j, i),
                    jnp.minimum(cb + jnp.minimum(j, n_tiles - 1) * n_parts + p, cb_max))))
            args.append(w)
    out_spec = pl.BlockSpec((tm, tn), lambda j, i: (row_tile(j, i), jnp.maximum(j - 1, 0)))
    if res is not None:
        in_specs.append(out_spec)
        args.append(res)
    out_specs = [out_spec]
    out_shape = [jax.ShapeDtypeStruct((m, n_out), out_dtype)]
    if emit_bf16:
        out_specs.append(out_spec)
        out_shape.append(jax.ShapeDtypeStruct((m, n_out), BF16))
    scratch = [pltpu.VMEM((2, kk, tn), BF16) for _ in ws]
    if norm_w is not None:
        scratch.append(pltpu.VMEM((n_m, tm, LANES), F32))
    kern = functools.partial(_dense_kernel, n_w=len(ws), n_parts=n_parts, n_tiles=n_tiles,
                             n_pieces=n_pieces, has_res=res is not None,
                             has_norm=norm_w is not None, emit_bf16=emit_bf16, epi=epi)
    outs = pl.pallas_call(
        kern,
        grid=(n_tiles + 1, n_m),
        in_specs=in_specs,
        out_specs=out_specs,
        out_shape=out_shape,
        scratch_shapes=scratch,
        compiler_params=_params(("arbitrary", "arbitrary")),
        name=name,
    )(*args)
    return outs if emit_bf16 else outs[0]


def _planes_to_rows_kernel(a_ref, b_ref, o_ref, *, n_a):
    def copy(src):
        for g in range(src.shape[0]):
            o_ref[:, g * LANES:(g + 1) * LANES] = src[g].astype(o_ref.dtype)

    @pl.when(pl.program_id(0) < n_a)
    def _():
        copy(a_ref)

    @pl.when(pl.program_id(0) >= n_a)
    def _():
        copy(b_ref)


def planes_to_rows(a_lg, b_lg, out_dtype):
    nlg, ma, _ = a_lg.shape
    mb = b_lg.shape[1]
    tm = _pick(math.gcd(ma, mb), (256, 128, 64, 32, 16))
    n_a, n_b = ma // tm, mb // tm
    return pl.pallas_call(
        functools.partial(_planes_to_rows_kernel, n_a=n_a),
        grid=(n_a + n_b,),
        in_specs=[pl.BlockSpec((nlg, tm, LANES), lambda i: (0, jnp.minimum(i, n_a - 1), 0)),
                  pl.BlockSpec((nlg, tm, LANES), lambda i: (0, jnp.maximum(i - n_a, 0), 0))],
        out_specs=pl.BlockSpec((tm, nlg * LANES), lambda i: (i, 0)),
        out_shape=jax.ShapeDtypeStruct((ma + mb, nlg * LANES), out_dtype),
        compiler_params=_params(("arbitrary",)),
        name="planes_to_rows",
    )(a_lg, b_lg)


def _s5_kernel(*refs, L, nc, n_seq):
    (u_ref, sre_ref, sim_ref, ar_ref, ai_ref, ldt_ref, br_ref, bi_ref, cr_ref, ci_ref, d_ref) = refs[:11]
    z_ref, ore_ref, oim_ref, tmat, qmat, pmat, binp, xst, xcat, xsb = refs[11:]
    H = S5_HALF

    ar = ar_ref[...]
    ai = ai_ref[...]
    dt = jnp.exp(ldt_ref[...])
    mag = jnp.exp(ar * dt)
    lr = mag * jnp.cos(ai * dt)
    li = mag * jnp.sin(ai * dt)
    den = ar * ar + ai * ai
    cor = ((lr - 1.0) * ar + li * ai) / den
    coi = (li * ar - (lr - 1.0) * ai) / den

    pows = [(jnp.ones_like(lr), jnp.zeros_like(lr))]
    for _ in range(L):
        pr, pi = pows[-1]
        pows.append((pr * lr - pi * li, pr * li + pi * lr))

    row_g = lax.broadcasted_iota(jnp.int32, (LANES, H), 0) // S5_GROUP
    lane_g = lax.broadcasted_iota(jnp.int32, (LANES, H), 1) // S5_STATE
    diag = row_g == lane_g
    br = jnp.where(diag, br_ref[...], 0.0)
    bi = jnp.where(diag, bi_ref[...], 0.0)
    cr = jnp.where(diag, cr_ref[...], 0.0)
    ci = jnp.where(diag, ci_ref[...], 0.0)
    a_r = br * cor - bi * coi
    a_i = br * coi + bi * cor
    a_cat = jnp.concatenate([a_r, a_i], axis=1).astype(BF16)
    zero_blk = jnp.zeros((LANES, LANES), BF16)
    for k in range(L + 1):
        pr, pi = pows[k]
        w_k = jnp.concatenate([cr * pr - ci * pi, -(cr * pi + ci * pr)], axis=1).astype(BF16)
        if k >= 1:
            pmat[(k - 1) * LANES:k * LANES, :] = w_k
        if k < L:
            d_k = lax.dot_general(a_cat, w_k, (((1,), (1,)), ((), ())),
                                  preferred_element_type=F32).astype(BF16)
            for s in range(L - k):
                t = s + k
                tmat[s * LANES:(s + 1) * LANES, t * LANES:(t + 1) * LANES] = d_k
            s = L - 1 - k
            qmat[s * LANES:(s + 1) * LANES, :] = jnp.concatenate(
                [a_r * pr - a_i * pi, a_r * pi + a_i * pr], axis=1).astype(BF16)
    for s in range(L):
        for t in range(s):
            tmat[s * LANES:(s + 1) * LANES, t * LANES:(t + 1) * LANES] = zero_blk

    hp = H // LANES
    lanes = lambda a, k: a[:, k * LANES:(k + 1) * LANES]
    ct = 2 * LANES if L % 2 == 0 else LANES

    n_rows = n_seq * nc
    pos = lambda s: pl.ds(s, n_rows, stride=L) if L > 1 else slice(None)

    for s in range(L):
        xcat[:, s * LANES:(s + 1) * LANES] = u_ref[pos(s), :].astype(BF16)

    slots = binp.shape[1] // nc
    seq_rows = lambda n: pl.ds(n, nc, stride=slots)
    for k0 in range(0, 2 * hp, 2):
        b2 = jnp.dot(xcat[...], qmat[:, k0 * LANES:(k0 + 2) * LANES], preferred_element_type=F32)
        if nc == 1:
            binp[k0] = b2[:, :LANES]
            binp[k0 + 1] = b2[:, LANES:]
        else:
            for n in range(n_seq):
                binp[k0, seq_rows(n), :] = b2[n * nc:(n + 1) * nc, :LANES]
                binp[k0 + 1, seq_rows(n), :] = b2[n * nc:(n + 1) * nc, LANES:]
    lLr, lLi = pows[L]
    lam_r = [lanes(lLr, k) for k in range(hp)]
    lam_i = [lanes(lLi, k) for k in range(hp)]
    x0r = sre_ref[...]
    x0i = sim_ref[...]

    if nc == 1:
        for k in range(hp):
            xr, xi = lanes(x0r, k), lanes(x0i, k)
            xsb[:, k * LANES:(k + 1) * LANES] = xr.astype(BF16)
            xsb[:, (hp + k) * LANES:(hp + k + 1) * LANES] = xi.astype(BF16)
            ore_ref[:, k * LANES:(k + 1) * LANES] = xr * lam_r[k] - xi * lam_i[k] + binp[k]
            oim_ref[:, k * LANES:(k + 1) * LANES] = xr * lam_i[k] + xi * lam_r[k] + binp[hp + k]
    else:
        def step(c, carry):
            rows = pl.ds(pl.multiple_of(c * slots, slots), n_seq)
            new = []
            for k in range(hp):
                xr, xi = carry[k], carry[hp + k]
                xst[k, rows, :] = xr
                xst[hp + k, rows, :] = xi
                new.append((xr * lam_r[k] - xi * lam_i[k] + binp[k, rows, :],
                            xr * lam_i[k] + xi * lam_r[k] + binp[hp + k, rows, :]))
            return tuple(r for r, _ in new) + tuple(i for _, i in new)

        init = tuple(lanes(x0r, k) for k in range(hp)) + tuple(lanes(x0i, k) for k in range(hp))
        fin = lax.fori_loop(0, nc, step, init, unroll=4)
        ore_ref[...] = jnp.concatenate(fin[:hp], axis=1)
        oim_ref[...] = jnp.concatenate(fin[hp:], axis=1)
        for k in range(2 * hp):
            for n in range(n_seq):
                xsb[n * nc:(n + 1) * nc, k * LANES:(k + 1) * LANES] = xst[k, seq_rows(n), :].astype(BF16)

    dsk = d_ref[...]
    for c0 in range(0, L * LANES, ct):
        kmax = c0 + ct
        y = jnp.dot(xcat[:, :kmax], tmat[:kmax, c0:c0 + ct], preferred_element_type=F32)
        y = y + lax.dot_general(xsb[...], pmat[c0:c0 + ct, :], (((1,), (1,)), ((), ())),
                                preferred_element_type=F32)
        for t in range(c0 // LANES, (c0 + ct) // LANES):
            yt = y[:, t * LANES - c0:(t + 1) * LANES - c0] + dsk * u_ref[pos(t), :]
            z_ref[pos(t), :] = jax.nn.gelu(yt)


def _s5_lane_rows(a):
    g, p = a.shape
    return a.reshape(g // S5_GPL, 1, S5_GPL * p)


def _s5_tiles(a):
    g, c, p = a.shape
    return jnp.tile(a.reshape(g // S5_GPL, S5_GPL * c, p), (1, 1, S5_GPL))


def s5_core(u_lg, row_start, n_seq, t_len, st_re, st_im,
            a_re, a_im, log_dt, b_re, b_im, c_re, c_im, d_skip):
    nlg, m, _ = u_lg.shape
    g, p = a_re.shape
    H = S5_HALF
    L = S5_CHUNK if t_len % S5_CHUNK == 0 else 1
    assert L > 1 or t_len == 1
    nc = t_len // L
    rows = n_seq * nc
    assert row_start % (rows * L) == 0 and m % L == 0
    rb = row_start // (rows * L)
    lane_args = [_s5_lane_rows(a_re), _s5_lane_rows(a_im),
                 _s5_lane_rows(jnp.broadcast_to(log_dt[:, None], (g, p)))]
    tile_args = [_s5_tiles(jnp.swapaxes(b_re, 1, 2)), _s5_tiles(jnp.swapaxes(b_im, 1, 2)),
                 _s5_tiles(c_re), _s5_tiles(c_im)]
    d_arg = d_skip.reshape(nlg, 1, LANES)
    lane_spec = pl.BlockSpec((None, 1, H), lambda j: (j, 0, 0))
    tile_spec = pl.BlockSpec((None, LANES, H), lambda j: (j, 0, 0))
    d_spec = pl.BlockSpec((None, 1, LANES), lambda j: (j, 0, 0))
    sublanes = 8
    plane_rows = rows if nc == 1 else nc * (-(-n_seq // sublanes) * sublanes)
    u_spec = pl.BlockSpec((None, rows * L, LANES), lambda j: (j, rb, 0))
    z_spec = pl.BlockSpec((None, rows * L, LANES), lambda j: (j, 0, 0))
    st_spec = pl.BlockSpec((n_seq, H), lambda j: (0, j))
    st_shape = jax.ShapeDtypeStruct((n_seq, g * p), F32)
    in_specs = [u_spec, st_spec, st_spec] + [lane_spec] * 3 + [tile_spec] * 4 + [d_spec]
    args = [u_lg, st_re, st_im, *lane_args, *tile_args, d_arg]
    kern = functools.partial(_s5_kernel, L=L, nc=nc, n_seq=n_seq)
    z, o_re, o_im = pl.pallas_call(
        kern,
        grid=(nlg,),
        in_specs=in_specs,
        out_specs=[z_spec, st_spec, st_spec],
        out_shape=[jax.ShapeDtypeStruct((nlg, rows * L, LANES), F32), st_shape, st_shape],
        scratch_shapes=[pltpu.VMEM((L * LANES, L * LANES), BF16),
                        pltpu.VMEM((L * LANES, 2 * H), BF16),
                        pltpu.VMEM((L * LANES, 2 * H), BF16),
                        pltpu.VMEM((2 * H // LANES, plane_rows, LANES), F32),
                        pltpu.VMEM((2 * H // LANES, plane_rows, LANES), F32),
                        pltpu.VMEM((rows, L * LANES), BF16),
                        pltpu.VMEM((rows, 2 * H), BF16)],
        compiler_params=_params(("arbitrary",)),
        name="s5_core",
    )(*args)
    return z, o_re, o_im


def _conv_prompt_kernel(gb_ref, gc_ref, v_ref, buf_ref, cw_ref, zprev_ref, z_ref, nb_ref):
    pre = gc_ref[...] * v_ref[...]
    t = pre.shape[0]
    buf = buf_ref[...]
    row = lax.broadcasted_iota(jnp.int32, pre.shape, 0)
    sh1 = jnp.where(row == 0, buf[1:2, :], pltpu.roll(pre, 1, 0))
    sh2 = pltpu.roll(pre, 2, 0)
    sh2 = jnp.where(row == 0, buf[0:1, :], jnp.where(row == 1, buf[1:2, :], sh2))
    cw = cw_ref[...]
    conv = cw[0:1, :] * sh2 + cw[1:2, :] * sh1 + cw[2:3, :] * pre
    z_ref[...] = (gb_ref[...] * conv).astype(z_ref.dtype)
    nb_ref[...] = pre[t - 2:t, :]


def conv_prompt(proj, z_prev, n_seq, t_len, buf, conv_w):
    d = proj.shape[1] // 3
    tc = _pick(d, (256, 128))
    ncb = d // tc
    return pl.pallas_call(
        _conv_prompt_kernel,
        grid=(n_seq, ncb),
        in_specs=[pl.BlockSpec((t_len, tc), lambda n, j: (n, j)),
                  pl.BlockSpec((t_len, tc), lambda n, j: (n, ncb + j)),
                  pl.BlockSpec((t_len, tc), lambda n, j: (n, 2 * ncb + j)),
                  pl.BlockSpec((None, 2, tc), lambda n, j: (n, 0, j)),
                  pl.BlockSpec((3, tc), lambda n, j: (0, j)),
                  pl.BlockSpec(memory_space=pl.ANY)],
        out_specs=[pl.BlockSpec((t_len, tc), lambda n, j: (n, j)),
                   pl.BlockSpec((None, 2, tc), lambda n, j: (n, 0, j))],
        out_shape=[jax.ShapeDtypeStruct(z_prev.shape, BF16),
                   jax.ShapeDtypeStruct((n_seq, 2, d), F32)],
        input_output_aliases={5: 0},
        compiler_params=_params(("arbitrary", "arbitrary")),
        name="conv_prompt",
    )(proj, proj, proj, buf, conv_w, z_prev)


def _conv_step_kernel(gb_ref, gc_ref, v_ref, b0_ref, b1_ref, cw_ref, zprev_ref, z_ref, n0_ref, n1_ref):
    pre = gc_ref[...] * v_ref[...]
    b0 = b0_ref[...]
    b1 = b1_ref[...]
    cw = cw_ref[...]
    conv = cw[0:1, :] * b0 + cw[1:2, :] * b1 + cw[2:3, :] * pre
    z_ref[...] = (gb_ref[...] * conv).astype(z_ref.dtype)
    n0_ref[...] = b1
    n1_ref[...] = pre


def conv_step(proj, z_prev, row_start, n_seq, buf, conv_w):
    d = proj.shape[1] // 3
    tc = _pick(d, (512, 256, 128))
    ncb = d // tc
    rb = row_start // n_seq
    buf2 = buf.reshape(n_seq, 2 * d)
    row_spec = lambda off: pl.BlockSpec((n_seq, tc), lambda j: (rb, off * ncb + j))
    b_spec = lambda off: pl.BlockSpec((n_seq, tc), lambda j: (0, off * ncb + j))
    o_spec = pl.BlockSpec((n_seq, tc), lambda j: (0, j))
    z, n0, n1 = pl.pallas_call(
        _conv_step_kernel,
        grid=(ncb,),
        in_specs=[row_spec(0), row_spec(1), row_spec(2), b_spec(0), b_spec(1),
                  pl.BlockSpec((3, tc), lambda j: (0, j)),
                  pl.BlockSpec(memory_space=pl.ANY)],
        out_specs=[pl.BlockSpec((n_seq, tc), lambda j: (rb, j)), o_spec, o_spec],
        out_shape=[jax.ShapeDtypeStruct(z_prev.shape, BF16),
                   jax.ShapeDtypeStruct((n_seq, d), F32),
                   jax.ShapeDtypeStruct((n_seq, d), F32)],
        input_output_aliases={6: 0},
        compiler_params=_params(("arbitrary",)),
        name="conv_step",
    )(proj, proj, proj, buf2, buf2, conv_w, z_prev)
    return z, jnp.stack([n0, n1], axis=1)


def _hgrn_lower_bound(logit_ref, layer):
    lg = logit_ref[...]
    rows = [lg[i:i + 1, :] for i in range(lg.shape[0])]
    mx = functools.reduce(jnp.maximum, rows)
    es = [jnp.exp(r - mx) for r in rows]
    tot = functools.reduce(lambda a, b: a + b, es)
    part = functools.reduce(lambda a, b: a + b, es[1:layer + 1])
    return part / tot


def _hgrn_gate_out(o, gn, g):
    o = o * lax.rsqrt(jnp.mean(o * o, axis=-1, keepdims=True) + RMS_EPS)
    return o * gn * jax.nn.sigmoid(g)


def _hgrn_prompt_kernel(q_ref, f_ref, v_ref, g_ref, lgt_ref, gn_ref, s0_ref, zprev_ref, z_ref, so_ref,
                        st_ref, *, layer, hb, tb, L):
    tblk = pl.program_id(2)
    lb = _hgrn_lower_bound(lgt_ref, layer)
    gn = gn_ref[...]

    @pl.when(tblk == 0)
    def _():
        for h in range(hb):
            st_ref[h] = s0_ref[0, h].T

    ri = lax.broadcasted_iota(jnp.int32, (L, L), 0)
    ci = lax.broadcasted_iota(jnp.int32, (L, L), 1)
    causal = ci <= ri
    tri = causal.astype(F32)
    mid = L // 2

    def chunk(c, carry):
        r0 = pl.multiple_of(c * L, L)
        fpre = f_ref[pl.ds(r0, L), :]
        f = lb + (1.0 - lb) * jax.nn.sigmoid(fpre)
        logf = jnp.log(f)
        kk = 1.0 - f
        b = jnp.dot(tri, logf, preferred_element_type=F32, precision=lax.Precision.HIGHEST)
        q = jax.nn.silu(q_ref[pl.ds(r0, L), :])
        v = v_ref[pl.ds(r0, L), :]
        g = g_ref[pl.ds(r0, L), :]
        b_end = b[L - 1:L, :]
        b_mid = b[mid - 1:mid, :]
        qs = q * jnp.exp(b - b_mid)
        ks = kk * jnp.exp(b_mid - b)
        qt = qs.astype(BF16)
        kt = ks.astype(BF16)
        qd = (qs * jnp.exp(b_mid)).astype(BF16)
        kd = (ks * jnp.exp(b_end - b_mid)).astype(BF16)
        vb = v.astype(BF16)
        dec = jnp.exp(b_end)
        outs = []
        for h in range(hb):
            sl = slice(h * HG_DK, (h + 1) * HG_DK)
            att = lax.dot_general(qt[:, sl], kt[:, sl], (((1,), (1,)), ((), ())),
                                  preferred_element_type=F32)
            att = jnp.where(causal, att, 0.0).astype(BF16)
            s_t = st_ref[h]
            o = jnp.dot(att, vb[:, sl], preferred_element_type=F32)
            o = o + lax.dot_general(qd[:, sl], s_t.astype(BF16), (((1,), (1,)), ((), ())),
                                    preferred_element_type=F32)
            upd = lax.dot_general(vb[:, sl], kd[:, sl], (((0,), (0,)), ((), ())),
                                  preferred_element_type=F32)
            st_ref[h] = s_t * dec[:, sl] + upd
            outs.append(_hgrn_gate_out(o, gn[:, sl], g[:, sl]))
        z_ref[pl.ds(r0, L), :] = jnp.concatenate(outs, axis=1).astype(z_ref.dtype)
        return carry

    lax.fori_loop(0, tb // L, chunk, 0, unroll=2)

    @pl.when(tblk == pl.num_programs(2) - 1)
    def _():
        for h in range(hb):
            so_ref[0, h] = st_ref[h].T


def hgrn_prompt(proj, z_prev, n_seq, t_len, s0, lb_logits, gnorm, layer):
    d = proj.shape[1] // 4
    heads = d // HG_DK
    hb = _pick(heads, (8, 4, 2, 1))
    nhb = heads // hb
    L = HG_CHUNK if t_len % HG_CHUNK == 0 else t_len
    tb = _pick(t_len, (512, 256, 128, 64))
    if tb % L:
        tb = t_len
    ntb = t_len // tb
    w = hb * HG_DK
    sec = lambda k: pl.BlockSpec((tb, w), lambda n, j, t: (n * ntb + t, k * nhb + j))
    kern = functools.partial(_hgrn_prompt_kernel, layer=layer, hb=hb, tb=tb, L=L)
    return pl.pallas_call(
        kern,
        grid=(n_seq, nhb, ntb),
        in_specs=[sec(0), sec(1), sec(2), sec(3),
                  pl.BlockSpec((lb_logits.shape[0], w), lambda n, j, t: (0, j)),
                  pl.BlockSpec((1, w), lambda n, j, t: (0, j)),
                  pl.BlockSpec((1, hb, HG_DK, HG_DK), lambda n, j, t: (n, j, 0, 0)),
                  pl.BlockSpec(memory_space=pl.ANY)],
        out_specs=[pl.BlockSpec((tb, w), lambda n, j, t: (n * ntb + t, j)),
                   pl.BlockSpec((1, hb, HG_DK, HG_DK), lambda n, j, t: (n, j, 0, 0))],
        out_shape=[jax.ShapeDtypeStruct(z_prev.shape, BF16),
                   jax.ShapeDtypeStruct(s0.shape, F32)],
        scratch_shapes=[pltpu.VMEM((hb, HG_DK, HG_DK), F32)],
        input_output_aliases={7: 0},
        compiler_params=_params(("arbitrary", "arbitrary", "arbitrary")),
        name="hgrn_prompt",
    )(proj, proj, proj, proj, lb_logits, gnorm.reshape(1, d), s0, z_prev)


def _hgrn_step_kernel(q_ref, f_ref, v_ref, g_ref, lgt_ref, gn_ref, s_ref, zprev_ref, z_ref, so_ref,
                      *, layer, nb):
    lb = _hgrn_lower_bound(lgt_ref, layer)
    f = lb + (1.0 - lb) * jax.nn.sigmoid(f_ref[...])
    q = jax.nn.silu(q_ref[...])
    v = v_ref[...]
    f_t = f.T
    k_t = 1.0 - f_t
    q_t = q.T
    outs = []
    for i in range(nb):
        s_new = f_t[:, i:i + 1] * s_ref[i, 0] + k_t[:, i:i + 1] * v[i:i + 1, :]
        so_ref[i, 0] = s_new
        outs.append(jnp.sum(q_t[:, i:i + 1] * s_new, axis=0, keepdims=True))
    o = jnp.concatenate(outs, axis=0)
    z_ref[...] = _hgrn_gate_out(o, gn_ref[...], g_ref[...]).astype(z_ref.dtype)


def hgrn_step(proj, z_prev, row_start, n_seq, s0, lb_logits, gnorm, layer):
    d = proj.shape[1] // 4
    heads = d // HG_DK
    nb = _pick(n_seq, (32, 16, 8))
    nnb = n_seq // nb
    rb = row_start // nb
    sec = lambda k: pl.BlockSpec((nb, HG_DK), lambda h, i: (rb + i, k * heads + h))
    s_spec = pl.BlockSpec((nb, 1, HG_DK, HG_DK), lambda h, i: (i, h, 0, 0))
    kern = functools.partial(_hgrn_step_kernel, layer=layer, nb=nb)
    return pl.pallas_call(
        kern,
        grid=(heads, nnb),
        in_specs=[sec(0), sec(1), sec(2), sec(3),
                  pl.BlockSpec((lb_logits.shape[0], HG_DK), lambda h, i: (0, h)),
                  pl.BlockSpec((1, HG_DK), lambda h, i: (0, h)),
                  s_spec,
                  pl.BlockSpec(memory_space=pl.ANY)],
        out_specs=[pl.BlockSpec((nb, HG_DK), lambda h, i: (rb + i, h)), s_spec],
        out_shape=[jax.ShapeDtypeStruct(z_prev.shape, BF16),
                   jax.ShapeDtypeStruct(s0.shape, F32)],
        input_output_aliases={7: 0},
        compiler_params=_params(("arbitrary", "arbitrary")),
        name="hgrn_step",
    )(proj, proj, proj, proj, lb_logits, gnorm.reshape(1, d), s0, z_prev)


def _row_tile(m, candidates):
    return _pick(m, candidates + (256, 128, 64, 32, 16))


def kernel(x_prompt, x_sample, state_s5_re, state_s5_im, state_conv, state_hgrn,
           norm_mix, norm_ffn, norm_final,
           s5_a_re, s5_a_im, s5_log_dt, s5_b_re, s5_b_im, s5_c_re, s5_c_im, s5_d, s5_w_glu,
           conv_w_in, conv_w, conv_w_out,
           hgrn_w_in, hgrn_lb_logits, hgrn_gnorm, hgrn_w_out,
           ffn_w_in, ffn_w_out):
    nb, t_len, d = x_prompt.shape
    ns = x_sample.shape[0]
    assert x_sample.shape[1] == 1
    mp = nb * t_len
    m = mp + ns
    depth = norm_mix.shape[0]
    g, p = s5_a_re.shape[1], s5_a_re.shape[2]
    gp = g * p
    dff = ffn_w_out.shape[1]
    heads = d // HG_DK
    tm_big = _row_tile(m, (1040,))
    tm_ffn = _row_tile(m, (832,))
    tn_big = _pick(d, (512, 256, 128))
    tnp_ff = _pick(dff, (256, 128))
    kk_ff = dff // 2 if (dff // 2) % LANES == 0 else dff

    h = jnp.concatenate([x_prompt.reshape(mp, d), x_sample.reshape(ns, d)], axis=0)
    hb = None
    z_state = jnp.zeros((nb, gp), F32)

    p_re, p_im, s_re, s_im, p_conv, s_conv, p_hg, s_hg = [], [], [], [], [], [], [], []
    for layer in range(depth):
        kind, j = layer % N_MIXERS, layer // N_MIXERS
        if kind == 0:
            u = rmsnorm_rows(h, norm_mix[layer], 0, m, F32, lane_groups=True)
            prm = (s5_a_re[j], s5_a_im[j], s5_log_dt[j], s5_b_re[j], s5_b_im[j],
                   s5_c_re[j], s5_c_im[j], s5_d[j])
            zp, pre_, pim_ = s5_core(u, 0, nb, t_len, z_state, z_state, *prm)
            zs, sre_, sim_ = s5_core(u, mp, ns, 1, state_s5_re[j].reshape(ns, gp),
                                     state_s5_im[j].reshape(ns, gp), *prm)
            p_re.append(pre_)
            p_im.append(pim_)
            s_re.append(sre_)
            s_im.append(sim_)
            tn = _pick(d, (256, 128))
            h, hb = dense(planes_to_rows(zp, zs, BF16), [(s5_w_glu, j, 0, 0), (s5_w_glu, j, 0, d // tn)], d,
                          epi=_epi_glu_res, out_dtype=F32, res=h, emit_bf16=True, tm=tm_ffn, tn=2 * tn,
                          n_parts=2, name="s5_glu")
        elif kind == 1:
            proj = dense(hb, [(conv_w_in, j, 0, 0)], 3 * d, epi=_epi_plain, out_dtype=F32,
                         norm_w=norm_mix[layer], tm=tm_ffn, tn=2 * tn_big, n_parts=2, name="conv_in")
            z, pb = conv_prompt(proj, jnp.zeros((m, d), BF16), nb, t_len, jnp.zeros((nb, 2, d), F32),
                                conv_w[j])
            z, sb = conv_step(proj, z, mp, ns, state_conv[j], conv_w[j])
            p_conv.append(pb)
            s_conv.append(sb)
            h, hb = dense(z, [(conv_w_out, j, 0, 0)], d, epi=_epi_res, out_dtype=F32, res=h,
                          emit_bf16=True, tm=tm_big, tn=tn_big, name="conv_out")
        else:
            proj = dense(hb, [(hgrn_w_in, j, 0, 0)], 4 * d, epi=_epi_plain, out_dtype=F32,
                         norm_w=norm_mix[layer], tm=tm_ffn, tn=2 * tn_big, n_parts=2, name="hgrn_in")
            z, ps = hgrn_prompt(proj, jnp.zeros((m, d), BF16), nb, t_len,
                                jnp.zeros((nb, heads, HG_DK, HG_DK), F32),
                                hgrn_lb_logits, hgrn_gnorm[j], layer)
            z, ss = hgrn_step(proj, z, mp, ns, state_hgrn[j], hgrn_lb_logits, hgrn_gnorm[j], layer)
            p_hg.append(ps)
            s_hg.append(ss)
            h, hb = dense(z, [(hgrn_w_out, j, 0, 0)], d, epi=_epi_res, out_dtype=F32, res=h,
                          emit_bf16=True, tm=tm_big, tn=tn_big, name="hgrn_out")
        hf = dense(hb, [(ffn_w_in, layer, 0, 0), (ffn_w_in, layer, 0, dff // tnp_ff)], dff,
                   epi=_epi_swiglu, out_dtype=BF16, norm_w=norm_ffn[layer], tm=tm_big,
                   tn=2 * tnp_ff, n_parts=2, name="ffn_in")
        n_kb = dff // kk_ff
        for kb in range(n_kb):
            last = kb == n_kb - 1
            out = dense(hf, [(ffn_w_out, layer, kb, 0)], d, epi=_epi_res, out_dtype=F32, res=h,
                        emit_bf16=last, tm=tm_big, tn=tn_big, kk=kk_ff, x_kblock=kb, name="ffn_out")
            h, hb = out if last else (out, None)

    y_prompt = rmsnorm_rows(h, norm_final, 0, mp, F32).reshape(nb, t_len, d)
    y_sample = rmsnorm_rows(h, norm_final, mp, ns, F32).reshape(ns, 1, d)
    return (y_prompt, y_sample,
            jnp.stack(p_re).reshape(-1, nb, g, p), jnp.stack(p_im).reshape(-1, nb, g, p),
            jnp.stack(p_conv), jnp.stack(p_hg),
            jnp.stack(s_re).reshape(-1, ns, g, p), jnp.stack(s_im).reshape(-1, ns, g, p),
            jnp.stack(s_conv), jnp.stack(s_hg))
```

```python
import functools
import math

import jax
import jax.numpy as jnp
from jax import lax
from jax.experimental import pallas as pl
from jax.experimental.pallas import tpu as pltpu

F32 = jnp.float32
BF16 = jnp.bfloat16

RMS_EPS = 1e-6
N_MIXERS = 3
S5_GROUP = 16
S5_STATE = 64
S5_CHUNK = 8
HG_DK = 128
HG_CHUNK = 64
LANES = 128
S5_GPL = LANES // S5_GROUP
S5_HALF = S5_GPL * S5_STATE
VMEM_LIMIT = 60 * 1024 * 1024


def _params(sem):
    return pltpu.CompilerParams(dimension_semantics=sem, vmem_limit_bytes=VMEM_LIMIT)


def _pick(dim, candidates):
    for c in candidates:
        if dim % c == 0:
            return c
    return dim


def _rms(x, w):
    return x * lax.rsqrt(jnp.mean(x * x, axis=-1, keepdims=True) + RMS_EPS) * w


def _rmsnorm_kernel(x_ref, w_ref, o_ref, *, lane_groups):
    tm = x_ref.shape[0]
    rc = _pick(tm, (64, 32, 16, 8))
    for r in range(0, tm, rc):
        y = _rms(x_ref[r:r + rc, :], w_ref[...]).astype(o_ref.dtype)
        if lane_groups:
            for g in range(o_ref.shape[0]):
                o_ref[g, r:r + rc, :] = y[:, g * LANES:(g + 1) * LANES]
        else:
            o_ref[r:r + rc, :] = y


def rmsnorm_rows(x, w, row_start, n_rows, out_dtype, lane_groups=False):
    d = x.shape[1]
    tm = _pick(math.gcd(row_start, n_rows) if row_start else n_rows, (320, 256, 128, 64, 32, 16, 8))
    off = row_start // tm
    if lane_groups:
        out_spec = pl.BlockSpec((d // LANES, tm, LANES), lambda i: (0, i, 0))
        out_shape = jax.ShapeDtypeStruct((d // LANES, n_rows, LANES), out_dtype)
    else:
        out_spec = pl.BlockSpec((tm, d), lambda i: (i, 0))
        out_shape = jax.ShapeDtypeStruct((n_rows, d), out_dtype)
    return pl.pallas_call(
        functools.partial(_rmsnorm_kernel, lane_groups=lane_groups),
        grid=(n_rows // tm,),
        in_specs=[pl.BlockSpec((tm, d), lambda i: (i + off, 0)),
                  pl.BlockSpec((1, d), lambda i: (0, 0))],
        out_specs=out_spec,
        out_shape=out_shape,
        compiler_params=_params(("arbitrary",)),
        name="rmsnorm",
    )(x, w.reshape(1, d))


def _epi_plain(accs, res):
    return accs[0]


def _epi_res(accs, res):
    return res + accs[0]


def _epi_glu_res(accs, res):
    return res + accs[0] * jax.nn.sigmoid(accs[1])


def _epi_swiglu(accs, res):
    return jax.nn.silu(accs[0]) * accs[1]


def _dense_kernel(*refs, n_w, n_parts, n_tiles, n_pieces, has_res, has_norm, emit_bf16, epi):
    it = iter(refs)
    x_ref = next(it)
    nw_ref = next(it) if has_norm else None
    w_refs = [[next(it) for _ in range(n_parts)] for _ in range(n_w)]
    res_ref = next(it) if has_res else None
    o_ref = next(it)
    ob_ref = next(it) if emit_bf16 else None
    wb_refs = [next(it) for _ in range(n_w)]
    rinv_ref = next(it) if has_norm else None
    j = pl.program_id(0)
    i = pl.program_id(1)
    kp, tnp = w_refs[0][0].shape

    @pl.when(jnp.logical_and(j < n_tiles, i < n_pieces))
    def _cast():
        r0 = pl.multiple_of(i * kp, kp)
        for parts, wb_ref in zip(w_refs, wb_refs):
            for p, w_ref in enumerate(parts):
                w = w_ref[...]
                if has_norm:
                    w = w * jnp.concatenate([nw_ref[...]] * (tnp // LANES), axis=1)
                wb_ref[j % 2, pl.ds(r0, kp), p * tnp:(p + 1) * tnp] = w.astype(BF16)

    if has_norm:
        @pl.when(j == 1)
        def _row_scale():
            tm, k = x_ref.shape
            rc = _pick(tm, (208, 128, 64, 32, 16))
            for r in range(0, tm, rc):
                xf = x_ref[r:r + rc, :].astype(F32)
                ssq = jnp.sum(xf * xf, axis=-1, keepdims=True)
                rinv_ref[i, r:r + rc, :] = jnp.broadcast_to(lax.rsqrt(ssq / k + RMS_EPS), (rc, LANES))

    @pl.when(j > 0)
    def _compute():
        slot = (j + 1) % 2
        xb = x_ref[...]
        for p in range(n_parts):
            cols = slice(p * tnp, (p + 1) * tnp)
            accs = [jnp.dot(xb, wb[slot, :, cols], preferred_element_type=F32) for wb in wb_refs]
            if has_norm:
                rinv = rinv_ref[i]
                scale = jnp.concatenate([rinv] * (tnp // LANES), axis=1) if tnp > LANES else rinv
                accs = [a * scale for a in accs]
            res = res_ref[:, cols] if has_res else None
            out = epi(accs, res)
            o_ref[:, cols] = out.astype(o_ref.dtype)
            if emit_bf16:
                ob_ref[:, cols] = out.astype(BF16)


def dense(x, ws, n_out, *, epi, out_dtype, tm, tn, n_parts=1, res=None, norm_w=None, emit_bf16=False,
          kk=None, x_kblock=0, name="dense"):
    m = x.shape[0]
    kk = kk or x.shape[1]
    assert m % tm == 0 and tn % n_parts == 0
    assert norm_w is None or kk == x.shape[1]
    n_m = m // tm
    n_tiles = -(-n_out // tn)
    n_pieces = 8
    assert n_m >= n_pieces and kk % (16 * n_pieces) == 0
    kp = kk // n_pieces
    tnp = tn // n_parts
    piece = lambda j, i: jnp.where(j >= n_tiles, n_pieces - 1, jnp.minimum(i, n_pieces - 1))
    row_tile = lambda j, i: jnp.where(j == 0, 0, i)
    in_specs = [pl.BlockSpec((tm, kk), lambda j, i: (row_tile(j, i), x_kblock))]
    args = [x]
    if norm_w is not None:
        in_specs.append(pl.BlockSpec((kp, LANES), lambda j, i: (piece(j, i), 0)))
        args.append(jnp.broadcast_to(norm_w[:, None], (kk, LANES)))
    for w, lyr, rb, cb in ws:
        cb_max = w.shape[2] // tnp - 1
        for p in range(n_parts):
            in_specs.append(pl.BlockSpec(
                (None, kp, tnp),
                lambda j, i, lyr=lyr, rb=rb, cb=cb, p=p, cb_max=cb_max: (
                    lyr, rb * n_pieces + piece(j, i),
                    jnp.minimum(cb + jnp.minimum(j, n_tiles - 1) * n_parts + p, cb_max))))
            args.append(w)
    out_spec = pl.BlockSpec((tm, tn), lambda j, i: (row_tile(j, i), jnp.maximum(j - 1, 0)))
    if res is not None:
        in_specs.append(out_spec)
        args.append(res)
    out_specs = [out_spec]
    out_shape = [jax.ShapeDtypeStruct((m, n_out), out_dtype)]
    if emit_bf16:
        out_specs.append(out_spec)
        out_shape.append(jax.ShapeDtypeStruct((m, n_out), BF16))
    scratch = [pltpu.VMEM((2, kk, tn), BF16) for _ in ws]
    if norm_w is not None:
        scratch.append(pltpu.VMEM((n_m, tm, LANES), F32))
    kern = functools.partial(_dense_kernel, n_w=len(ws), n_parts=n_parts, n_tiles=n_tiles,
                             n_pieces=n_pieces, has_res=res is not None,
                             has_norm=norm_w is not None, emit_bf16=emit_bf16, epi=epi)
    outs = pl.pallas_call(
        kern,
        grid=(n_tiles + 1, n_m),
        in_specs=in_specs,
        out_specs=out_specs,
        out_shape=out_shape,
        scratch_shapes=scratch,
        compiler_params=_params(("arbitrary", "arbitrary")),
        name=name,
    )(*args)
    return outs if emit_bf16 else outs[0]


def _planes_to_rows_kernel(a_ref, b_ref, o_ref, *, n_a):
    def copy(src):
        for g in range(src.shape[0]):
            o_ref[:, g * LANES:(g + 1) * LANES] = src[g].astype(o_ref.dtype)

    @pl.when(pl.program_id(0) < n_a)
    def _():
        copy(a_ref)

    @pl.when(pl.program_id(0) >= n_a)
    def _():
        copy(b_ref)


def planes_to_rows(a_lg, b_lg, out_dtype):
    nlg, ma, _ = a_lg.shape
    mb = b_lg.shape[1]
    tm = _pick(math.gcd(ma, mb), (256, 128, 64, 32, 16))
    n_a, n_b = ma // tm, mb // tm
    return pl.pallas_call(
        functools.partial(_planes_to_rows_kernel, n_a=n_a),
        grid=(n_a + n_b,),
        in_specs=[pl.BlockSpec((nlg, tm, LANES), lambda i: (0, jnp.minimum(i, n_a - 1), 0)),
                  pl.BlockSpec((nlg, tm, LANES), lambda i: (0, jnp.maximum(i - n_a, 0), 0))],
        out_specs=pl.BlockSpec((tm, nlg * LANES), lambda i: (i, 0)),
        out_shape=jax.ShapeDtypeStruct((ma + mb, nlg * LANES), out_dtype),
        compiler_params=_params(("arbitrary",)),
        name="planes_to_rows",
    )(a_lg, b_lg)


def _s5_kernel(*refs, L, nc, n_seq):
    (u_ref, sre_ref, sim_ref, ar_ref, ai_ref, ldt_ref, br_ref, bi_ref, cr_ref, ci_ref, d_ref) = refs[:11]
    z_ref, ore_ref, oim_ref, tmat, qmat, pmat, binp, xst, xcat, xsb = refs[11:]
    H = S5_HALF

    ar = ar_ref[...]
    ai = ai_ref[...]
    dt = jnp.exp(ldt_ref[...])
    mag = jnp.exp(ar * dt)
    lr = mag * jnp.cos(ai * dt)
    li = mag * jnp.sin(ai * dt)
    den = ar * ar + ai * ai
    cor = ((lr - 1.0) * ar + li * ai) / den
    coi = (li * ar - (lr - 1.0) * ai) / den

    pows = [(jnp.ones_like(lr), jnp.zeros_like(lr))]
    for _ in range(L):
        pr, pi = pows[-1]
        pows.append((pr * lr - pi * li, pr * li + pi * lr))

    row_g = lax.broadcasted_iota(jnp.int32, (LANES, H), 0) // S5_GROUP
    lane_g = lax.broadcasted_iota(jnp.int32, (LANES, H), 1) // S5_STATE
    diag = row_g == lane_g
    br = jnp.where(diag, br_ref[...], 0.0)
    bi = jnp.where(diag, bi_ref[...], 0.0)
    cr = jnp.where(diag, cr_ref[...], 0.0)
    ci = jnp.where(diag, ci_ref[...], 0.0)
    a_r = br * cor - bi * coi
    a_i = br * coi + bi * cor
    a_cat = jnp.concatenate([a_r, a_i], axis=1).astype(BF16)
    zero_blk = jnp.zeros((LANES, LANES), BF16)
    for k in range(L + 1):
        pr, pi = pows[k]
        w_k = jnp.concatenate([cr * pr - ci * pi, -(cr * pi + ci * pr)], axis=1).astype(BF16)
        if k >= 1:
            pmat[(k - 1) * LANES:k * LANES, :] = w_k
        if k < L:
            d_k = lax.dot_general(a_cat, w_k, (((1,), (1,)), ((), ())),
                                  preferred_element_type=F32).astype(BF16)
            for s in range(L - k):
                t = s + k
                tmat[s * LANES:(s + 1) * LANES, t * LANES:(t + 1) * LANES] = d_k
            s = L - 1 - k
            qmat[s * LANES:(s + 1) * LANES, :] = jnp.concatenate(
                [a_r * pr - a_i * pi, a_r * pi + a_i * pr], axis=1).astype(BF16)
    for s in range(L):
        for t in range(s):
            tmat[s * LANES:(s + 1) * LANES, t * LANES:(t + 1) * LANES] = zero_blk

    hp = H // LANES
    lanes = lambda a, k: a[:, k * LANES:(k + 1) * LANES]
    ct = 2 * LANES if L % 2 == 0 else LANES

    n_rows = n_seq * nc
    pos = lambda s: pl.ds(s, n_rows, stride=L) if L > 1 else slice(None)

    for s in range(L):
        xcat[:, s * LANES:(s + 1) * LANES] = u_ref[pos(s), :].astype(BF16)

    slots = binp.shape[1] // nc
    seq_rows = lambda n: pl.ds(n, nc, stride=slots)
    for k0 in range(0, 2 * hp, 2):
        b2 = jnp.dot(xcat[...], qmat[:, k0 * LANES:(k0 + 2) * LANES], preferred_element_type=F32)
        if nc == 1:
            binp[k0] = b2[:, :LANES]
            binp[k0 + 1] = b2[:, LANES:]
        else:
            for n in range(n_seq):
                binp[k0, seq_rows(n), :] = b2[n * nc:(n + 1) * nc, :LANES]
                binp[k0 + 1, seq_rows(n), :] = b2[n * nc:(n + 1) * nc, LANES:]
    lLr, lLi = pows[L]
    lam_r = [lanes(lLr, k) for k in range(hp)]
    lam_i = [lanes(lLi, k) for k in range(hp)]
    x0r = sre_ref[...]
    x0i = sim_ref[...]

    if nc == 1:
        for k in range(hp):
            xr, xi = lanes(x0r, k), lanes(x0i, k)
            xsb[:, k * LANES:(k + 1) * LANES] = xr.astype(BF16)
            xsb[:, (hp + k) * LANES:(hp + k + 1) * LANES] = xi.astype(BF16)
            ore_ref[:, k * LANES:(k + 1) * LANES] = xr * lam_r[k] - xi * lam_i[k] + binp[k]
            oim_ref[:, k * LANES:(k + 1) * LANES] = xr * lam_i[k] + xi * lam_r[k] + binp[hp + k]
    else:
        def step(c, carry):
            rows = pl.ds(pl.multiple_of(c * slots, slots), n_seq)
            new = []
            for k in range(hp):
                xr, xi = carry[k], carry[hp + k]
                xst[k, rows, :] = xr
                xst[hp + k, rows, :] = xi
                new.append((xr * lam_r[k] - xi * lam_i[k] + binp[k, rows, :],
                            xr * lam_i[k] + xi * lam_r[k] + binp[hp + k, rows, :]))
            return tuple(r for r, _ in new) + tuple(i for _, i in new)

        init = tuple(lanes(x0r, k) for k in range(hp)) + tuple(lanes(x0i, k) for k in range(hp))
        fin = lax.fori_loop(0, nc, step, init, unroll=4)
        ore_ref[...] = jnp.concatenate(fin[:hp], axis=1)
        oim_ref[...] = jnp.concatenate(fin[hp:], axis=1)
        for k in range(2 * hp):
            for n in range(n_seq):
                xsb[n * nc:(n + 1) * nc, k * LANES:(k + 1) * LANES] = xst[k, seq_rows(n), :].astype(BF16)

    dsk = d_ref[...]
    for c0 in range(0, L * LANES, ct):
        kmax = c0 + ct
        y = jnp.dot(xcat[:, :kmax], tmat[:kmax, c0:c0 + ct], preferred_element_type=F32)
        y = y + lax.dot_general(xsb[...], pmat[c0:c0 + ct, :], (((1,), (1,)), ((), ())),
                                preferred_element_type=F32)
        for t in range(c0 // LANES, (c0 + ct) // LANES):
            yt = y[:, t * LANES - c0:(t + 1) * LANES - c0] + dsk * u_ref[pos(t), :]
            z_ref[pos(t), :] = jax.nn.gelu(yt)


def _s5_lane_rows(a):
    g, p = a.shape
    return a.reshape(g // S5_GPL, 1, S5_GPL * p)


def _s5_tiles(a):
    g, c, p = a.shape
    return jnp.tile(a.reshape(g // S5_GPL, S5_GPL * c, p), (1, 1, S5_GPL))


def s5_core(u_lg, row_start, n_seq, t_len, st_re, st_im,
            a_re, a_im, log_dt, b_re, b_im, c_re, c_im, d_skip):
    nlg, m, _ = u_lg.shape
    g, p = a_re.shape
    H = S5_HALF
    L = S5_CHUNK if t_len % S5_CHUNK == 0 else 1
    assert L > 1 or t_len == 1
    nc = t_len // L
    rows = n_seq * nc
    assert row_start % (rows * L) == 0 and m % L == 0
    rb = row_start // (rows * L)
    lane_args = [_s5_lane_rows(a_re), _s5_lane_rows(a_im),
                 _s5_lane_rows(jnp.broadcast_to(log_dt[:, None], (g, p)))]
    tile_args = [_s5_tiles(jnp.swapaxes(b_re, 1, 2)), _s5_tiles(jnp.swapaxes(b_im, 1, 2)),
                 _s5_tiles(c_re), _s5_tiles(c_im)]
    d_arg = d_skip.reshape(nlg, 1, LANES)
    lane_spec = pl.BlockSpec((None, 1, H), lambda j: (j, 0, 0))
    tile_spec = pl.BlockSpec((None, LANES, H), lambda j: (j, 0, 0))
    d_spec = pl.BlockSpec((None, 1, LANES), lambda j: (j, 0, 0))
    sublanes = 8
    plane_rows = rows if nc == 1 else nc * (-(-n_seq // sublanes) * sublanes)
    u_spec = pl.BlockSpec((None, rows * L, LANES), lambda j: (j, rb, 0))
    z_spec = pl.BlockSpec((None, rows * L, LANES), lambda j: (j, 0, 0))
    st_spec = pl.BlockSpec((n_seq, H), lambda j: (0, j))
    st_shape = jax.ShapeDtypeStruct((n_seq, g * p), F32)
    in_specs = [u_spec, st_spec, st_spec] + [lane_spec] * 3 + [tile_spec] * 4 + [d_spec]
    args = [u_lg, st_re, st_im, *lane_args, *tile_args, d_arg]
    kern = functools.partial(_s5_kernel, L=L, nc=nc, n_seq=n_seq)
    z, o_re, o_im = pl.pallas_call(
        kern,
        grid=(nlg,),
        in_specs=in_specs,
        out_specs=[z_spec, st_spec, st_spec],
        out_shape=[jax.ShapeDtypeStruct((nlg, rows * L, LANES), F32), st_shape, st_shape],
        scratch_shapes=[pltpu.VMEM((L * LANES, L * LANES), BF16),
                        pltpu.VMEM((L * LANES, 2 * H), BF16),
                        pltpu.VMEM((L * LANES, 2 * H), BF16),
                        pltpu.VMEM((2 * H // LANES, plane_rows, LANES), F32),
                        pltpu.VMEM((2 * H // LANES, plane_rows, LANES), F32),
                        pltpu.VMEM((rows, L * LANES), BF16),
                        pltpu.VMEM((rows, 2 * H), BF16)],
        compiler_params=_params(("arbitrary",)),
        name="s5_core",
    )(*args)
    return z, o_re, o_im


def _conv_prompt_kernel(gb_ref, gc_ref, v_ref, buf_ref, cw_ref, zprev_ref, z_ref, nb_ref):
    pre = gc_ref[...] * v_ref[...]
    t = pre.shape[0]
    buf = buf_ref[...]
    row = lax.broadcasted_iota(jnp.int32, pre.shape, 0)
    sh1 = jnp.where(row == 0, buf[1:2, :], pltpu.roll(pre, 1, 0))
    sh2 = pltpu.roll(pre, 2, 0)
    sh2 = jnp.where(row == 0, buf[0:1, :], jnp.where(row == 1, buf[1:2, :], sh2))
    cw = cw_ref[...]
    conv = cw[0:1, :] * sh2 + cw[1:2, :] * sh1 + cw[2:3, :] * pre
    z_ref[...] = (gb_ref[...] * conv).astype(z_ref.dtype)
    nb_ref[...] = pre[t - 2:t, :]


def conv_prompt(proj, z_prev, n_seq, t_len, buf, conv_w):
    d = proj.shape[1] // 3
    tc = _pick(d, (256, 128))
    ncb = d // tc
    return pl.pallas_call(
        _conv_prompt_kernel,
        grid=(n_seq, ncb),
        in_specs=[pl.BlockSpec((t_len, tc), lambda n, j: (n, j)),
                  pl.BlockSpec((t_len, tc), lambda n, j: (n, ncb + j)),
                  pl.BlockSpec((t_len, tc), lambda n, j: (n, 2 * ncb + j)),
                  pl.BlockSpec((None, 2, tc), lambda n, j: (n, 0, j)),
                  pl.BlockSpec((3, tc), lambda n, j: (0, j)),
                  pl.BlockSpec(memory_space=pl.ANY)],
        out_specs=[pl.BlockSpec((t_len, tc), lambda n, j: (n, j)),
                   pl.BlockSpec((None, 2, tc), lambda n, j: (n, 0, j))],
        out_shape=[jax.ShapeDtypeStruct(z_prev.shape, BF16),
                   jax.ShapeDtypeStruct((n_seq, 2, d), F32)],
        input_output_aliases={5: 0},
        compiler_params=_params(("arbitrary", "arbitrary")),
        name="conv_prompt",
    )(proj, proj, proj, buf, conv_w, z_prev)


def _conv_step_kernel(gb_ref, gc_ref, v_ref, b0_ref, b1_ref, cw_ref, zprev_ref, z_ref, n0_ref, n1_ref):
    pre = gc_ref[...] * v_ref[...]
    b0 = b0_ref[...]
    b1 = b1_ref[...]
    cw = cw_ref[...]
    conv = cw[0:1, :] * b0 + cw[1:2, :] * b1 + cw[2:3, :] * pre
    z_ref[...] = (gb_ref[...] * conv).astype(z_ref.dtype)
    n0_ref[...] = b1
    n1_ref[...] = pre


def conv_step(proj, z_prev, row_start, n_seq, buf, conv_w):
    d = proj.shape[1] // 3
    tc = _pick(d, (512, 256, 128))
    ncb = d // tc
    rb = row_start // n_seq
    buf2 = buf.reshape(n_seq, 2 * d)
    row_spec = lambda off: pl.BlockSpec((n_seq, tc), lambda j: (rb, off * ncb + j))
    b_spec = lambda off: pl.BlockSpec((n_seq, tc), lambda j: (0, off * ncb + j))
    o_spec = pl.BlockSpec((n_seq, tc), lambda j: (0, j))
    z, n0, n1 = pl.pallas_call(
        _conv_step_kernel,
        grid=(ncb,),
        in_specs=[row_spec(0), row_spec(1), row_spec(2), b_spec(0), b_spec(1),
                  pl.BlockSpec((3, tc), lambda j: (0, j)),
                  pl.BlockSpec(memory_space=pl.ANY)],
        out_specs=[pl.BlockSpec((n_seq, tc), lambda j: (rb, j)), o_spec, o_spec],
        out_shape=[jax.ShapeDtypeStruct(z_prev.shape, BF16),
                   jax.ShapeDtypeStruct((n_seq, d), F32),
                   jax.ShapeDtypeStruct((n_seq, d), F32)],
        input_output_aliases={6: 0},
        compiler_params=_params(("arbitrary",)),
        name="conv_step",
    )(proj, proj, proj, buf2, buf2, conv_w, z_prev)
    return z, jnp.stack([n0, n1], axis=1)


def _hgrn_lower_bound(logit_ref, layer):
    lg = logit_ref[...]
    rows = [lg[i:i + 1, :] for i in range(lg.shape[0])]
    mx = functools.reduce(jnp.maximum, rows)
    es = [jnp.exp(r - mx) for r in rows]
    tot = functools.reduce(lambda a, b: a + b, es)
    part = functools.reduce(lambda a, b: a + b, es[1:layer + 1])
    return part / tot


def _hgrn_gate_out(o, gn, g):
    o = o * lax.rsqrt(jnp.mean(o * o, axis=-1, keepdims=True) + RMS_EPS)
    return o * gn * jax.nn.sigmoid(g)


def _hgrn_prompt_kernel(q_ref, f_ref, v_ref, g_ref, lgt_ref, gn_ref, s0_ref, zprev_ref, z_ref, so_ref,
                        st_ref, *, layer, hb, tb, L):
    tblk = pl.program_id(2)
    lb = _hgrn_lower_bound(lgt_ref, layer)
    gn = gn_ref[...]

    @pl.when(tblk == 0)
    def _():
        for h in range(hb):
            st_ref[h] = s0_ref[0, h].T

    ri = lax.broadcasted_iota(jnp.int32, (L, L), 0)
    ci = lax.broadcasted_iota(jnp.int32, (L, L), 1)
    causal = ci <= ri
    tri = causal.astype(F32)
    mid = L // 2

    def chunk(c, carry):
        r0 = pl.multiple_of(c * L, L)
        fpre = f_ref[pl.ds(r0, L), :]
        f = lb + (1.0 - lb) * jax.nn.sigmoid(fpre)
        logf = jnp.log(f)
        kk = 1.0 - f
        b = jnp.dot(tri, logf, preferred_element_type=F32, precision=lax.Precision.HIGHEST)
        q = jax.nn.silu(q_ref[pl.ds(r0, L), :])
        v = v_ref[pl.ds(r0, L), :]
        g = g_ref[pl.ds(r0, L), :]
        b_end = b[L - 1:L, :]
        b_mid = b[mid - 1:mid, :]
        qs = q * jnp.exp(b - b_mid)
        ks = kk * jnp.exp(b_mid - b)
        qt = qs.astype(BF16)
        kt = ks.astype(BF16)
        qd = (qs * jnp.exp(b_mid)).astype(BF16)
        kd = (ks * jnp.exp(b_end - b_mid)).astype(BF16)
        vb = v.astype(BF16)
        dec = jnp.exp(b_end)
        outs = []
        for h in range(hb):
            sl = slice(h * HG_DK, (h + 1) * HG_DK)
            att = lax.dot_general(qt[:, sl], kt[:, sl], (((1,), (1,)), ((), ())),
                                  preferred_element_type=F32)
            att = jnp.where(causal, att, 0.0).astype(BF16)
            s_t = st_ref[h]
            o = jnp.dot(att, vb[:, sl], preferred_element_type=F32)
            o = o + lax.dot_general(qd[:, sl], s_t.astype(BF16), (((1,), (1,)), ((), ())),
                                    preferred_element_type=F32)
            upd = lax.dot_general(vb[:, sl], kd[:, sl], (((0,), (0,)), ((), ())),
                                  preferred_element_type=F32)
            st_ref[h] = s_t * dec[:, sl] + upd
            outs.append(_hgrn_gate_out(o, gn[:, sl], g[:, sl]))
        z_ref[pl.ds(r0, L), :] = jnp.concatenate(outs, axis=1).astype(z_ref.dtype)
        return carry

    lax.fori_loop(0, tb // L, chunk, 0, unroll=True)

    @pl.when(tblk == pl.num_programs(2) - 1)
    def _():
        for h in range(hb):
            so_ref[0, h] = st_ref[h].T


def hgrn_prompt(proj, z_prev, n_seq, t_len, s0, lb_logits, gnorm, layer):
    d = proj.shape[1] // 4
    heads = d // HG_DK
    hb = _pick(heads, (8, 4, 2, 1))
    nhb = heads // hb
    L = HG_CHUNK if t_len % HG_CHUNK == 0 else t_len
    tb = _pick(t_len, (512, 256, 128, 64))
    if tb % L:
        tb = t_len
    ntb = t_len // tb
    w = hb * HG_DK
    sec = lambda k: pl.BlockSpec((tb, w), lambda n, j, t: (n * ntb + t, k * nhb + j))
    kern = functools.partial(_hgrn_prompt_kernel, layer=layer, hb=hb, tb=tb, L=L)
    return pl.pallas_call(
        kern,
        grid=(n_seq, nhb, ntb),
        in_specs=[sec(0), sec(1), sec(2), sec(3),
                  pl.BlockSpec((lb_logits.shape[0], w), lambda n, j, t: (0, j)),
                  pl.BlockSpec((1, w), lambda n, j, t: (0, j)),
                  pl.BlockSpec((1, hb, HG_DK, HG_DK), lambda n, j, t: (n, j, 0, 0)),
                  pl.BlockSpec(memory_space=pl.ANY)],
        out_specs=[pl.BlockSpec((tb, w), lambda n, j, t: (n * ntb + t, j)),
                   pl.BlockSpec((1, hb, HG_DK, HG_DK), lambda n, j, t: (n, j, 0, 0))],
        out_shape=[jax.ShapeDtypeStruct(z_prev.shape, BF16),
                   jax.ShapeDtypeStruct(s0.shape, F32)],
        scratch_shapes=[pltpu.VMEM((hb, HG_DK, HG_DK), F32)],
        input_output_aliases={7: 0},
        compiler_params=_params(("arbitrary", "arbitrary", "arbitrary")),
        name="hgrn_prompt",
    )(proj, proj, proj, proj, lb_logits, gnorm.reshape(1, d), s0, z_prev)


def _hgrn_step_kernel(q_ref, f_ref, v_ref, g_ref, lgt_ref, gn_ref, s_ref, zprev_ref, z_ref, so_ref,
                      *, layer, nb):
    lb = _hgrn_lower_bound(lgt_ref, layer)
    f = lb + (1.0 - lb) * jax.nn.sigmoid(f_ref[...])
    q = jax.nn.silu(q_ref[...])
    v = v_ref[...]
    f_t = f.T
    k_t = 1.0 - f_t
    qb = q.astype(BF16)
    outs = []
    for i in range(nb):
        s_new = f_t[:, i:i + 1] * s_ref[i, 0] + k_t[:, i:i + 1] * v[i:i + 1, :]
        so_ref[i, 0] = s_new
        outs.append(jnp.dot(qb[i:i + 1, :], s_new.astype(BF16), preferred_element_type=F32))
    o = jnp.concatenate(outs, axis=0)
    z_ref[...] = _hgrn_gate_out(o, gn_ref[...], g_ref[...]).astype(z_ref.dtype)


def hgrn_step(proj, z_prev, row_start, n_seq, s0, lb_logits, gnorm, layer):
    d = proj.shape[1] // 4
    heads = d // HG_DK
    nb = _pick(n_seq, (32, 16, 8))
    nnb = n_seq // nb
    rb = row_start // nb
    sec = lambda k: pl.BlockSpec((nb, HG_DK), lambda h, i: (rb + i, k * heads + h))
    s_spec = pl.BlockSpec((nb, 1, HG_DK, HG_DK), lambda h, i: (i, h, 0, 0))
    kern = functools.partial(_hgrn_step_kernel, layer=layer, nb=nb)
    return pl.pallas_call(
        kern,
        grid=(heads, nnb),
        in_specs=[sec(0), sec(1), sec(2), sec(3),
                  pl.BlockSpec((lb_logits.shape[0], HG_DK), lambda h, i: (0, h)),
                  pl.BlockSpec((1, HG_DK), lambda h, i: (0, h)),
                  s_spec,
                  pl.BlockSpec(memory_space=pl.ANY)],
        out_specs=[pl.BlockSpec((nb, HG_DK), lambda h, i: (rb + i, h)), s_spec],
        out_shape=[jax.ShapeDtypeStruct(z_prev.shape, BF16),
                   jax.ShapeDtypeStruct(s0.shape, F32)],
        input_output_aliases={7: 0},
        compiler_params=_params(("arbitrary", "arbitrary")),
        name="hgrn_step",
    )(proj, proj, proj, proj, lb_logits, gnorm.reshape(1, d), s0, z_prev)


def _row_tile(m, candidates):
    return _pick(m, candidates + (256, 128, 64, 32, 16))


def kernel(x_prompt, x_sample, state_s5_re, state_s5_im, state_conv, state_hgrn,
           norm_mix, norm_ffn, norm_final,
           s5_a_re, s5_a_im, s5_log_dt, s5_b_re, s5_b_im, s5_c_re, s5_c_im, s5_d, s5_w_glu,
           conv_w_in, conv_w, conv_w_out,
           hgrn_w_in, hgrn_lb_logits, hgrn_gnorm, hgrn_w_out,
           ffn_w_in, ffn_w_out):
    nb, t_len, d = x_prompt.shape
    ns = x_sample.shape[0]
    assert x_sample.shape[1] == 1
    mp = nb * t_len
    m = mp + ns
    depth = norm_mix.shape[0]
    g, p = s5_a_re.shape[1], s5_a_re.shape[2]
    gp = g * p
    dff = ffn_w_out.shape[1]
    heads = d // HG_DK
    tm_big = _row_tile(m, (1040,))
    tm_ffn = _row_tile(m, (832,))
    tn_big = _pick(d, (512, 256, 128))
    tnp_ff = _pick(dff, (256, 128))
    kk_ff = dff // 2 if (dff // 2) % LANES == 0 else dff

    h = jnp.concatenate([x_prompt.reshape(mp, d), x_sample.reshape(ns, d)], axis=0)
    hb = None
    z_state = jnp.zeros((nb, gp), F32)

    p_re, p_im, s_re, s_im, p_conv, s_conv, p_hg, s_hg = [], [], [], [], [], [], [], []
    for layer in range(depth):
        kind, j = layer % N_MIXERS, layer // N_MIXERS
        if kind == 0:
            u = rmsnorm_rows(h, norm_mix[layer], 0, m, F32, lane_groups=True)
            prm = (s5_a_re[j], s5_a_im[j], s5_log_dt[j], s5_b_re[j], s5_b_im[j],
                   s5_c_re[j], s5_c_im[j], s5_d[j])
            zp, pre_, pim_ = s5_core(u, 0, nb, t_len, z_state, z_state, *prm)
            zs, sre_, sim_ = s5_core(u, mp, ns, 1, state_s5_re[j].reshape(ns, gp),
                                     state_s5_im[j].reshape(ns, gp), *prm)
            p_re.append(pre_)
            p_im.append(pim_)
            s_re.append(sre_)
            s_im.append(sim_)
            tn = _pick(d, (256, 128))
            h, hb = dense(planes_to_rows(zp, zs, BF16), [(s5_w_glu, j, 0, 0), (s5_w_glu, j, 0, d // tn)], d,
                          epi=_epi_glu_res, out_dtype=F32, res=h, emit_bf16=True, tm=tm_ffn, tn=2 * tn,
                          n_parts=2, name="s5_glu")
        elif kind == 1:
            proj = dense(hb, [(conv_w_in, j, 0, 0)], 3 * d, epi=_epi_plain, out_dtype=F32,
                         norm_w=norm_mix[layer], tm=tm_ffn, tn=2 * tn_big, n_parts=2, name="conv_in")
            z, pb = conv_prompt(proj, jnp.zeros((m, d), BF16), nb, t_len, jnp.zeros((nb, 2, d), F32),
                                conv_w[j])
            z, sb = conv_step(proj, z, mp, ns, state_conv[j], conv_w[j])
            p_conv.append(pb)
            s_conv.append(sb)
            h, hb = dense(z, [(conv_w_out, j, 0, 0)], d, epi=_epi_res, out_dtype=F32, res=h,
                          emit_bf16=True, tm=tm_big, tn=tn_big, name="conv_out")
        else:
            proj = dense(hb, [(hgrn_w_in, j, 0, 0)], 4 * d, epi=_epi_plain, out_dtype=F32,
                         norm_w=norm_mix[layer], tm=tm_ffn, tn=2 * tn_big, n_parts=2, name="hgrn_in")
            z, ps = hgrn_prompt(proj, jnp.zeros((m, d), BF16), nb, t_len,
                                jnp.zeros((nb, heads, HG_DK, HG_DK), F32),
                                hgrn_lb_logits, hgrn_gnorm[j], layer)
            z, ss = hgrn_step(proj, z, mp, ns, state_hgrn[j], hgrn_lb_logits, hgrn_gnorm[j], layer)
            p_hg.append(ps)
            s_hg.append(ss)
            h, hb = dense(z, [(hgrn_w_out, j, 0, 0)], d, epi=_epi_res, out_dtype=F32, res=h,
                          emit_bf16=True, tm=tm_big, tn=tn_big, name="hgrn_out")
        hf = dense(hb, [(ffn_w_in, layer, 0, 0), (ffn_w_in, layer, 0, dff // tnp_ff)], dff,
                   epi=_epi_swiglu, out_dtype=BF16, norm_w=norm_ffn[layer], tm=tm_big,
                   tn=2 * tnp_ff, n_parts=2, name="ffn_in")
        n_kb = dff // kk_ff
        for kb in range(n_kb):
            last = kb == n_kb - 1
            out = dense(hf, [(ffn_w_out, layer, kb, 0)], d, epi=_epi_res, out_dtype=F32, res=h,
                        emit_bf16=last, tm=tm_big, tn=tn_big, kk=kk_ff, x_kblock=kb, name="ffn_out")
            h, hb = out if last else (out, None)

    y_prompt = rmsnorm_rows(h, norm_final, 0, mp, F32).reshape(nb, t_len, d)
    y_sample = rmsnorm_rows(h, norm_final, mp, ns, F32).reshape(ns, 1, d)
    return (y_prompt, y_sample,
            jnp.stack(p_re).reshape(-1, nb, g, p), jnp.stack(p_im).reshape(-1, nb, g, p),
            jnp.stack(p_conv), jnp.stack(p_hg),
            jnp.stack(s_re).reshape(-1, ns, g, p), jnp.stack(s_im).reshape(-1, ns, g, p),
            jnp.stack(s_conv), jnp.stack(s_hg))
```

```python
import functools
import math

import jax
import jax.numpy as jnp
from jax import lax
from jax.experimental import pallas as pl
from jax.experimental.pallas import tpu as pltpu

F32 = jnp.float32
BF16 = jnp.bfloat16

RMS_EPS = 1e-6
N_MIXERS = 3
S5_GROUP = 16
S5_STATE = 64
S5_CHUNK = 8
HG_DK = 128
HG_CHUNK = 64
LANES = 128
S5_GPL = LANES // S5_GROUP
S5_HALF = S5_GPL * S5_STATE
VMEM_LIMIT = 60 * 1024 * 1024


def _params(sem):
    return pltpu.CompilerParams(dimension_semantics=sem, vmem_limit_bytes=VMEM_LIMIT)


def _pick(dim, candidates):
    for c in candidates:
        if dim % c == 0:
            return c
    return dim


def _rms(x, w):
    return x * lax.rsqrt(jnp.mean(x * x, axis=-1, keepdims=True) + RMS_EPS) * w


def _rmsnorm_kernel(x_ref, w_ref, o_ref, *, lane_groups):
    tm = x_ref.shape[0]
    rc = _pick(tm, (64, 32, 16, 8))
    for r in range(0, tm, rc):
        y = _rms(x_ref[r:r + rc, :], w_ref[...]).astype(o_ref.dtype)
        if lane_groups:
            for g in range(o_ref.shape[0]):
                o_ref[g, r:r + rc, :] = y[:, g * LANES:(g + 1) * LANES]
        else:
            o_ref[r:r + rc, :] = y


def rmsnorm_rows(x, w, row_start, n_rows, out_dtype, lane_groups=False):
    d = x.shape[1]
    tm = _pick(math.gcd(row_start, n_rows) if row_start else n_rows, (320, 256, 128, 64, 32, 16, 8))
    off = row_start // tm
    if lane_groups:
        out_spec = pl.BlockSpec((d // LANES, tm, LANES), lambda i: (0, i, 0))
        out_shape = jax.ShapeDtypeStruct((d // LANES, n_rows, LANES), out_dtype)
    else:
        out_spec = pl.BlockSpec((tm, d), lambda i: (i, 0))
        out_shape = jax.ShapeDtypeStruct((n_rows, d), out_dtype)
    return pl.pallas_call(
        functools.partial(_rmsnorm_kernel, lane_groups=lane_groups),
        grid=(n_rows // tm,),
        in_specs=[pl.BlockSpec((tm, d), lambda i: (i + off, 0)),
                  pl.BlockSpec((1, d), lambda i: (0, 0))],
        out_specs=out_spec,
        out_shape=out_shape,
        compiler_params=_params(("arbitrary",)),
        name="rmsnorm",
    )(x, w.reshape(1, d))


def _stack_norm_kernel(a_ref, b_ref, w_ref, h_ref, u_ref, *, n_a):
    def run(src):
        tm = src.shape[0]
        rc = _pick(tm, (64, 32, 16, 8))
        for r in range(0, tm, rc):
            x = src[r:r + rc, :]
            h_ref[r:r + rc, :] = x
            y = _rms(x, w_ref[...])
            for g in range(u_ref.shape[0]):
                u_ref[g, r:r + rc, :] = y[:, g * LANES:(g + 1) * LANES]

    @pl.when(pl.program_id(0) < n_a)
    def _():
        run(a_ref)

    @pl.when(pl.program_id(0) >= n_a)
    def _():
        run(b_ref)


def stack_and_norm(a, b, w):
    ma, d = a.shape
    mb = b.shape[0]
    tm = _pick(math.gcd(ma, mb), (128, 64, 32, 16, 8))
    n_a, n_b = ma // tm, mb // tm
    return pl.pallas_call(
        functools.partial(_stack_norm_kernel, n_a=n_a),
        grid=(n_a + n_b,),
        in_specs=[pl.BlockSpec((tm, d), lambda i: (jnp.minimum(i, n_a - 1), 0)),
                  pl.BlockSpec((tm, d), lambda i: (jnp.maximum(i - n_a, 0), 0)),
                  pl.BlockSpec((1, d), lambda i: (0, 0))],
        out_specs=[pl.BlockSpec((tm, d), lambda i: (i, 0)),
                   pl.BlockSpec((d // LANES, tm, LANES), lambda i: (0, i, 0))],
        out_shape=[jax.ShapeDtypeStruct((ma + mb, d), F32),
                   jax.ShapeDtypeStruct((d // LANES, ma + mb, LANES), F32)],
        compiler_params=_params(("arbitrary",)),
        name="stack_and_norm",
    )(a, b, w.reshape(1, d))


def _epi_plain(accs, res):
    return accs[0]


def _epi_res(accs, res):
    return res + accs[0]


def _epi_glu_res(accs, res):
    return res + accs[0] * jax.nn.sigmoid(accs[1])


def _epi_swiglu(accs, res):
    return jax.nn.silu(accs[0]) * accs[1]


def _dense_kernel(*refs, n_w, n_parts, n_tiles, n_pieces, has_res, has_norm, emit_bf16, epi):
    it = iter(refs)
    x_ref = next(it)
    nw_ref = next(it) if has_norm else None
    w_refs = [[next(it) for _ in range(n_parts)] for _ in range(n_w)]
    res_ref = next(it) if has_res else None
    o_ref = next(it)
    ob_ref = next(it) if emit_bf16 else None
    wb_refs = [next(it) for _ in range(n_w)]
    rinv_ref = next(it) if has_norm else None
    j = pl.program_id(0)
    i = pl.program_id(1)
    kp, tnp = w_refs[0][0].shape

    @pl.when(jnp.logical_and(j < n_tiles, i < n_pieces))
    def _cast():
        r0 = pl.multiple_of(i * kp, kp)
        for parts, wb_ref in zip(w_refs, wb_refs):
            for p, w_ref in enumerate(parts):
                w = w_ref[...]
                if has_norm:
                    w = w * jnp.concatenate([nw_ref[...]] * (tnp // LANES), axis=1)
                wb_ref[j % 2, pl.ds(r0, kp), p * tnp:(p + 1) * tnp] = w.astype(BF16)

    if has_norm:
        @pl.when(j == 1)
        def _row_scale():
            tm, k = x_ref.shape
            rc = _pick(tm, (208, 128, 64, 32, 16))
            for r in range(0, tm, rc):
                xf = x_ref[r:r + rc, :].astype(F32)
                ssq = jnp.sum(xf * xf, axis=-1, keepdims=True)
                rinv_ref[i, r:r + rc, :] = jnp.broadcast_to(lax.rsqrt(ssq / k + RMS_EPS), (rc, LANES))

    @pl.when(j > 0)
    def _compute():
        slot = (j + 1) % 2
        xb = x_ref[...]
        for p in range(n_parts):
            cols = slice(p * tnp, (p + 1) * tnp)
            accs = [jnp.dot(xb, wb[slot, :, cols], preferred_element_type=F32) for wb in wb_refs]
            if has_norm:
                rinv = rinv_ref[i]
                scale = jnp.concatenate([rinv] * (tnp // LANES), axis=1) if tnp > LANES else rinv
                accs = [a * scale for a in accs]
            res = res_ref[:, cols] if has_res else None
            out = epi(accs, res)
            o_ref[:, cols] = out.astype(o_ref.dtype)
            if emit_bf16:
                ob_ref[:, cols] = out.astype(BF16)


def dense(x, ws, n_out, *, epi, out_dtype, tm, tn, n_parts=1, res=None, norm_w=None, emit_bf16=False,
          kk=None, x_kblock=0, name="dense"):
    m = x.shape[0]
    kk = kk or x.shape[1]
    assert m % tm == 0 and tn % n_parts == 0
    assert norm_w is None or kk == x.shape[1]
    n_m = m // tm
    n_tiles = -(-n_out // tn)
    n_pieces = 8
    assert n_m >= n_pieces and kk % (16 * n_pieces) == 0
    kp = kk // n_pieces
    tnp = tn // n_parts
    piece = lambda j, i: jnp.where(j >= n_tiles, n_pieces - 1, jnp.minimum(i, n_pieces - 1))
    row_tile = lambda j, i: jnp.where(j == 0, 0, i)
    in_specs = [pl.BlockSpec((tm, kk), lambda j, i: (row_tile(j, i), x_kblock))]
    args = [x]
    if norm_w is not None:
        in_specs.append(pl.BlockSpec((kp, LANES), lambda j, i: (piece(j, i), 0)))
        args.append(jnp.broadcast_to(norm_w[:, None], (kk, LANES)))
    for w, lyr, rb, cb in ws:
        cb_max = w.shape[2] // tnp - 1
        for p in range(n_parts):
            in_specs.append(pl.BlockSpec(
                (None, kp, tnp),
                lambda j, i, lyr=lyr, rb=rb, cb=cb, p=p, cb_max=cb_max: (
                    lyr, rb * n_pieces + piece(j, i),
                    jnp.minimum(cb + jnp.minimum(j, n_tiles - 1) * n_parts + p, cb_max))))
            args.append(w)
    out_spec = pl.BlockSpec((tm, tn), lambda j, i: (row_tile(j, i), jnp.maximum(j - 1, 0)))
    if res is not None:
        in_specs.append(out_spec)
        args.append(res)
    out_specs = [out_spec]
    out_shape = [jax.ShapeDtypeStruct((m, n_out), out_dtype)]
    if emit_bf16:
        out_specs.append(out_spec)
        out_shape.append(jax.ShapeDtypeStruct((m, n_out), BF16))
    scratch = [pltpu.VMEM((2, kk, tn), BF16) for _ in ws]
    if norm_w is not None:
        scratch.append(pltpu.VMEM((n_m, tm, LANES), F32))
    kern = functools.partial(_dense_kernel, n_w=len(ws), n_parts=n_parts, n_tiles=n_tiles,
                             n_pieces=n_pieces, has_res=res is not None,
                             has_norm=norm_w is not None, emit_bf16=emit_bf16, epi=epi)
    outs = pl.pallas_call(
        kern,
        grid=(n_tiles + 1, n_m),
        in_specs=in_specs,
        out_specs=out_specs,
        out_shape=out_shape,
        scratch_shapes=scratch,
        compiler_params=_params(("arbitrary", "arbitrary")),
        name=name,
    )(*args)
    return outs if emit_bf16 else outs[0]


def _planes_to_rows_kernel(a_ref, b_ref, o_ref, *, n_a):
    def copy(src):
        for g in range(src.shape[0]):
            o_ref[:, g * LANES:(g + 1) * LANES] = src[g].astype(o_ref.dtype)

    @pl.when(pl.program_id(0) < n_a)
    def _():
        copy(a_ref)

    @pl.when(pl.program_id(0) >= n_a)
    def _():
        copy(b_ref)


def planes_to_rows(a_lg, b_lg, out_dtype):
    nlg, ma, _ = a_lg.shape
    mb = b_lg.shape[1]
    tm = _pick(math.gcd(ma, mb), (256, 128, 64, 32, 16))
    n_a, n_b = ma // tm, mb // tm
    return pl.pallas_call(
        functools.partial(_planes_to_rows_kernel, n_a=n_a),
        grid=(n_a + n_b,),
        in_specs=[pl.BlockSpec((nlg, tm, LANES), lambda i: (0, jnp.minimum(i, n_a - 1), 0)),
                  pl.BlockSpec((nlg, tm, LANES), lambda i: (0, jnp.maximum(i - n_a, 0), 0))],
        out_specs=pl.BlockSpec((tm, nlg * LANES), lambda i: (i, 0)),
        out_shape=jax.ShapeDtypeStruct((ma + mb, nlg * LANES), out_dtype),
        compiler_params=_params(("arbitrary",)),
        name="planes_to_rows",
    )(a_lg, b_lg)


def _s5_kernel(*refs, L, nc, n_seq):
    (u_ref, sre_ref, sim_ref, ar_ref, ai_ref, ldt_ref, br_ref, bi_ref, cr_ref, ci_ref, d_ref) = refs[:11]
    z_ref, ore_ref, oim_ref, tmat, qmat, pmat, binp, xst, xcat, xsb = refs[11:]
    H = S5_HALF

    ar = ar_ref[...]
    ai = ai_ref[...]
    dt = jnp.exp(ldt_ref[...])
    mag = jnp.exp(ar * dt)
    lr = mag * jnp.cos(ai * dt)
    li = mag * jnp.sin(ai * dt)
    den = ar * ar + ai * ai
    cor = ((lr - 1.0) * ar + li * ai) / den
    coi = (li * ar - (lr - 1.0) * ai) / den

    pows = [(jnp.ones_like(lr), jnp.zeros_like(lr))]
    for _ in range(L):
        pr, pi = pows[-1]
        pows.append((pr * lr - pi * li, pr * li + pi * lr))

    row_g = lax.broadcasted_iota(jnp.int32, (LANES, H), 0) // S5_GROUP
    lane_g = lax.broadcasted_iota(jnp.int32, (LANES, H), 1) // S5_STATE
    diag = row_g == lane_g
    br = jnp.where(diag, br_ref[...], 0.0)
    bi = jnp.where(diag, bi_ref[...], 0.0)
    cr = jnp.where(diag, cr_ref[...], 0.0)
    ci = jnp.where(diag, ci_ref[...], 0.0)
    a_r = br * cor - bi * coi
    a_i = br * coi + bi * cor
    a_cat = jnp.concatenate([a_r, a_i], axis=1).astype(BF16)
    zero_blk = jnp.zeros((LANES, LANES), BF16)
    for k in range(L + 1):
        pr, pi = pows[k]
        w_k = jnp.concatenate([cr * pr - ci * pi, -(cr * pi + ci * pr)], axis=1).astype(BF16)
        if k >= 1:
            pmat[(k - 1) * LANES:k * LANES, :] = w_k
        if k < L:
            d_k = lax.dot_general(a_cat, w_k, (((1,), (1,)), ((), ())),
                                  preferred_element_type=F32).astype(BF16)
            for s in range(L - k):
                t = s + k
                tmat[s * LANES:(s + 1) * LANES, t * LANES:(t + 1) * LANES] = d_k
            s = L - 1 - k
            qmat[s * LANES:(s + 1) * LANES, :] = jnp.concatenate(
                [a_r * pr - a_i * pi, a_r * pi + a_i * pr], axis=1).astype(BF16)
    for s in range(L):
        for t in range(s):
            tmat[s * LANES:(s + 1) * LANES, t * LANES:(t + 1) * LANES] = zero_blk

    hp = H // LANES
    lanes = lambda a, k: a[:, k * LANES:(k + 1) * LANES]
    ct = 2 * LANES if L % 2 == 0 else LANES

    n_rows = n_seq * nc
    pos = lambda s: pl.ds(s, n_rows, stride=L) if L > 1 else slice(None)

    for s in range(L):
        xcat[:, s * LANES:(s + 1) * LANES] = u_ref[pos(s), :].astype(BF16)

    slots = binp.shape[1] // nc
    seq_rows = lambda n: pl.ds(n, nc, stride=slots)
    for k0 in range(0, 2 * hp, 2):
        b2 = jnp.dot(xcat[...], qmat[:, k0 * LANES:(k0 + 2) * LANES], preferred_element_type=F32)
        if nc == 1:
            binp[k0] = b2[:, :LANES]
            binp[k0 + 1] = b2[:, LANES:]
        else:
            for n in range(n_seq):
                binp[k0, seq_rows(n), :] = b2[n * nc:(n + 1) * nc, :LANES]
                binp[k0 + 1, seq_rows(n), :] = b2[n * nc:(n + 1) * nc, LANES:]
    lLr, lLi = pows[L]
    lam_r = [lanes(lLr, k) for k in range(hp)]
    lam_i = [lanes(lLi, k) for k in range(hp)]
    x0r = sre_ref[...]
    x0i = sim_ref[...]

    if nc == 1:
        for k in range(hp):
            xr, xi = lanes(x0r, k), lanes(x0i, k)
            xsb[:, k * LANES:(k + 1) * LANES] = xr.astype(BF16)
            xsb[:, (hp + k) * LANES:(hp + k + 1) * LANES] = xi.astype(BF16)
            ore_ref[:, k * LANES:(k + 1) * LANES] = xr * lam_r[k] - xi * lam_i[k] + binp[k]
            oim_ref[:, k * LANES:(k + 1) * LANES] = xr * lam_i[k] + xi * lam_r[k] + binp[hp + k]
    else:
        def step(c, carry):
            rows = pl.ds(pl.multiple_of(c * slots, slots), n_seq)
            new = []
            for k in range(hp):
                xr, xi = carry[k], carry[hp + k]
                xst[k, rows, :] = xr
                xst[hp + k, rows, :] = xi
                new.append((xr * lam_r[k] - xi * lam_i[k] + binp[k, rows, :],
                            xr * lam_i[k] + xi * lam_r[k] + binp[hp + k, rows, :]))
            return tuple(r for r, _ in new) + tuple(i for _, i in new)

        init = tuple(lanes(x0r, k) for k in range(hp)) + tuple(lanes(x0i, k) for k in range(hp))
        fin = lax.fori_loop(0, nc, step, init, unroll=4)
        ore_ref[...] = jnp.concatenate(fin[:hp], axis=1)
        oim_ref[...] = jnp.concatenate(fin[hp:], axis=1)
        for k in range(2 * hp):
            for n in range(n_seq):
                xsb[n * nc:(n + 1) * nc, k * LANES:(k + 1) * LANES] = xst[k, seq_rows(n), :].astype(BF16)

    dsk = d_ref[...]
    for c0 in range(0, L * LANES, ct):
        kmax = c0 + ct
        y = jnp.dot(xcat[:, :kmax], tmat[:kmax, c0:c0 + ct], preferred_element_type=F32)
        y = y + lax.dot_general(xsb[...], pmat[c0:c0 + ct, :], (((1,), (1,)), ((), ())),
                                preferred_element_type=F32)
        for t in range(c0 // LANES, (c0 + ct) // LANES):
            yt = y[:, t * LANES - c0:(t + 1) * LANES - c0] + dsk * u_ref[pos(t), :]
            z_ref[pos(t), :] = jax.nn.gelu(yt)


def _s5_lane_rows(a):
    g, p = a.shape
    return a.reshape(g // S5_GPL, 1, S5_GPL * p)


def _s5_tiles(a):
    g, c, p = a.shape
    return jnp.tile(a.reshape(g // S5_GPL, S5_GPL * c, p), (1, 1, S5_GPL))


def s5_core(u_lg, row_start, n_seq, t_len, st_re, st_im,
            a_re, a_im, log_dt, b_re, b_im, c_re, c_im, d_skip):
    nlg, m, _ = u_lg.shape
    g, p = a_re.shape
    H = S5_HALF
    L = S5_CHUNK if t_len % S5_CHUNK == 0 else 1
    assert L > 1 or t_len == 1
    nc = t_len // L
    rows = n_seq * nc
    assert row_start % (rows * L) == 0 and m % L == 0
    rb = row_start // (rows * L)
    lane_args = [_s5_lane_rows(a_re), _s5_lane_rows(a_im),
                 _s5_lane_rows(jnp.broadcast_to(log_dt[:, None], (g, p)))]
    tile_args = [_s5_tiles(jnp.swapaxes(b_re, 1, 2)), _s5_tiles(jnp.swapaxes(b_im, 1, 2)),
                 _s5_tiles(c_re), _s5_tiles(c_im)]
    d_arg = d_skip.reshape(nlg, 1, LANES)
    lane_spec = pl.BlockSpec((None, 1, H), lambda j: (j, 0, 0))
    tile_spec = pl.BlockSpec((None, LANES, H), lambda j: (j, 0, 0))
    d_spec = pl.BlockSpec((None, 1, LANES), lambda j: (j, 0, 0))
    sublanes = 8
    plane_rows = rows if nc == 1 else nc * (-(-n_seq // sublanes) * sublanes)
    u_spec = pl.BlockSpec((None, rows * L, LANES), lambda j: (j, rb, 0))
    z_spec = pl.BlockSpec((None, rows * L, LANES), lambda j: (j, 0, 0))
    st_spec = pl.BlockSpec((n_seq, H), lambda j: (0, j))
    st_shape = jax.ShapeDtypeStruct((n_seq, g * p), F32)
    in_specs = [u_spec, st_spec, st_spec] + [lane_spec] * 3 + [tile_spec] * 4 + [d_spec]
    args = [u_lg, st_re, st_im, *lane_args, *tile_args, d_arg]
    kern = functools.partial(_s5_kernel, L=L, nc=nc, n_seq=n_seq)
    z, o_re, o_im = pl.pallas_call(
        kern,
        grid=(nlg,),
        in_specs=in_specs,
        out_specs=[z_spec, st_spec, st_spec],
        out_shape=[jax.ShapeDtypeStruct((nlg, rows * L, LANES), F32), st_shape, st_shape],
        scratch_shapes=[pltpu.VMEM((L * LANES, L * LANES), BF16),
                        pltpu.VMEM((L * LANES, 2 * H), BF16),
                        pltpu.VMEM((L * LANES, 2 * H), BF16),
                        pltpu.VMEM((2 * H // LANES, plane_rows, LANES), F32),
                        pltpu.VMEM((2 * H // LANES, plane_rows, LANES), F32),
                        pltpu.VMEM((rows, L * LANES), BF16),
                        pltpu.VMEM((rows, 2 * H), BF16)],
        compiler_params=_params(("arbitrary",)),
        name="s5_core",
    )(*args)
    return z, o_re, o_im


def _conv_prompt_kernel(gb_ref, gc_ref, v_ref, buf_ref, cw_ref, zprev_ref, z_ref, nb_ref):
    pre = gc_ref[...] * v_ref[...]
    t = pre.shape[0]
    buf = buf_ref[...]
    row = lax.broadcasted_iota(jnp.int32, pre.shape, 0)
    sh1 = jnp.where(row == 0, buf[1:2, :], pltpu.roll(pre, 1, 0))
    sh2 = pltpu.roll(pre, 2, 0)
    sh2 = jnp.where(row == 0, buf[0:1, :], jnp.where(row == 1, buf[1:2, :], sh2))
    cw = cw_ref[...]
    conv = cw[0:1, :] * sh2 + cw[1:2, :] * sh1 + cw[2:3, :] * pre
    z_ref[...] = (gb_ref[...] * conv).astype(z_ref.dtype)
    nb_ref[...] = pre[t - 2:t, :]


def conv_prompt(proj, z_prev, n_seq, t_len, buf, conv_w):
    d = proj.shape[1] // 3
    tc = _pick(d, (256, 128))
    ncb = d // tc
    return pl.pallas_call(
        _conv_prompt_kernel,
        grid=(n_seq, ncb),
        in_specs=[pl.BlockSpec((t_len, tc), lambda n, j: (n, j)),
                  pl.BlockSpec((t_len, tc), lambda n, j: (n, ncb + j)),
                  pl.BlockSpec((t_len, tc), lambda n, j: (n, 2 * ncb + j)),
                  pl.BlockSpec((None, 2, tc), lambda n, j: (n, 0, j)),
                  pl.BlockSpec((3, tc), lambda n, j: (0, j)),
                  pl.BlockSpec(memory_space=pl.ANY)],
        out_specs=[pl.BlockSpec((t_len, tc), lambda n, j: (n, j)),
                   pl.BlockSpec((None, 2, tc), lambda n, j: (n, 0, j))],
        out_shape=[jax.ShapeDtypeStruct(z_prev.shape, BF16),
                   jax.ShapeDtypeStruct((n_seq, 2, d), F32)],
        input_output_aliases={5: 0},
        compiler_params=_params(("arbitrary", "arbitrary")),
        name="conv_prompt",
    )(proj, proj, proj, buf, conv_w, z_prev)


def _conv_step_kernel(gb_ref, gc_ref, v_ref, b0_ref, b1_ref, cw_ref, zprev_ref, z_ref, n0_ref, n1_ref):
    pre = gc_ref[...] * v_ref[...]
    b0 = b0_ref[...]
    b1 = b1_ref[...]
    cw = cw_ref[...]
    conv = cw[0:1, :] * b0 + cw[1:2, :] * b1 + cw[2:3, :] * pre
    z_ref[...] = (gb_ref[...] * conv).astype(z_ref.dtype)
    n0_ref[...] = b1
    n1_ref[...] = pre


def conv_step(proj, z_prev, row_start, n_seq, buf, conv_w):
    d = proj.shape[1] // 3
    tc = _pick(d, (512, 256, 128))
    ncb = d // tc
    rb = row_start // n_seq
    buf2 = buf.reshape(n_seq, 2 * d)
    row_spec = lambda off: pl.BlockSpec((n_seq, tc), lambda j: (rb, off * ncb + j))
    b_spec = lambda off: pl.BlockSpec((n_seq, tc), lambda j: (0, off * ncb + j))
    o_spec = pl.BlockSpec((n_seq, tc), lambda j: (0, j))
    z, n0, n1 = pl.pallas_call(
        _conv_step_kernel,
        grid=(ncb,),
        in_specs=[row_spec(0), row_spec(1), row_spec(2), b_spec(0), b_spec(1),
                  pl.BlockSpec((3, tc), lambda j: (0, j)),
                  pl.BlockSpec(memory_space=pl.ANY)],
        out_specs=[pl.BlockSpec((n_seq, tc), lambda j: (rb, j)), o_spec, o_spec],
        out_shape=[jax.ShapeDtypeStruct(z_prev.shape, BF16),
                   jax.ShapeDtypeStruct((n_seq, d), F32),
                   jax.ShapeDtypeStruct((n_seq, d), F32)],
        input_output_aliases={6: 0},
        compiler_params=_params(("arbitrary",)),
        name="conv_step",
    )(proj, proj, proj, buf2, buf2, conv_w, z_prev)
    return z, jnp.stack([n0, n1], axis=1)


def _hgrn_lower_bound(logit_ref, layer):
    lg = logit_ref[...]
    rows = [lg[i:i + 1, :] for i in range(lg.shape[0])]
    mx = functools.reduce(jnp.maximum, rows)
    es = [jnp.exp(r - mx) for r in rows]
    tot = functools.reduce(lambda a, b: a + b, es)
    part = functools.reduce(lambda a, b: a + b, es[1:layer + 1])
    return part / tot


def _hgrn_gate_out(o, gn, g):
    o = o * lax.rsqrt(jnp.mean(o * o, axis=-1, keepdims=True) + RMS_EPS)
    return o * gn * jax.nn.sigmoid(g)


def _hgrn_prompt_kernel(q_ref, f_ref, v_ref, g_ref, lgt_ref, gn_ref, s0_ref, zprev_ref, z_ref, so_ref,
                        st_ref, *, layer, hb, tb, L):
    tblk = pl.program_id(2)
    lb = _hgrn_lower_bound(lgt_ref, layer)
    gn = gn_ref[...]

    @pl.when(tblk == 0)
    def _():
        for h in range(hb):
            st_ref[h] = s0_ref[0, h].T

    ri = lax.broadcasted_iota(jnp.int32, (L, L), 0)
    ci = lax.broadcasted_iota(jnp.int32, (L, L), 1)
    causal = ci <= ri
    tri = causal.astype(F32)
    mid = L // 2

    def chunk(c, carry):
        r0 = pl.multiple_of(c * L, L)
        fpre = f_ref[pl.ds(r0, L), :]
        f = lb + (1.0 - lb) * jax.nn.sigmoid(fpre)
        logf = jnp.log(f)
        kk = 1.0 - f
        b = jnp.dot(tri, logf, preferred_element_type=F32, precision=lax.Precision.HIGHEST)
        q = jax.nn.silu(q_ref[pl.ds(r0, L), :])
        v = v_ref[pl.ds(r0, L), :]
        g = g_ref[pl.ds(r0, L), :]
        b_end = b[L - 1:L, :]
        b_mid = b[mid - 1:mid, :]
        qs = q * jnp.exp(b - b_mid)
        ks = kk * jnp.exp(b_mid - b)
        qt = qs.astype(BF16)
        kt = ks.astype(BF16)
        qd = (qs * jnp.exp(b_mid)).astype(BF16)
        kd = (ks * jnp.exp(b_end - b_mid)).astype(BF16)
        vb = v.astype(BF16)
        dec = jnp.exp(b_end)
        outs = []
        for h in range(hb):
            sl = slice(h * HG_DK, (h + 1) * HG_DK)
            att = lax.dot_general(qt[:, sl], kt[:, sl], (((1,), (1,)), ((), ())),
                                  preferred_element_type=F32)
            att = jnp.where(causal, att, 0.0).astype(BF16)
            s_t = st_ref[h]
            o = jnp.dot(att, vb[:, sl], preferred_element_type=F32)
            o = o + lax.dot_general(qd[:, sl], s_t.astype(BF16), (((1,), (1,)), ((), ())),
                                    preferred_element_type=F32)
            upd = lax.dot_general(vb[:, sl], kd[:, sl], (((0,), (0,)), ((), ())),
                                  preferred_element_type=F32)
            st_ref[h] = s_t * dec[:, sl] + upd
            outs.append(_hgrn_gate_out(o, gn[:, sl], g[:, sl]))
        z_ref[pl.ds(r0, L), :] = jnp.concatenate(outs, axis=1).astype(z_ref.dtype)
        return carry

    lax.fori_loop(0, tb // L, chunk, 0, unroll=True)

    @pl.when(tblk == pl.num_programs(2) - 1)
    def _():
        for h in range(hb):
            so_ref[0, h] = st_ref[h].T


def hgrn_prompt(proj, z_prev, n_seq, t_len, s0, lb_logits, gnorm, layer):
    d = proj.shape[1] // 4
    heads = d // HG_DK
    hb = _pick(heads, (8, 4, 2, 1))
    nhb = heads // hb
    L = HG_CHUNK if t_len % HG_CHUNK == 0 else t_len
    tb = _pick(t_len, (512, 256, 128, 64))
    if tb % L:
        tb = t_len
    ntb = t_len // tb
    w = hb * HG_DK
    sec = lambda k: pl.BlockSpec((tb, w), lambda n, j, t: (n * ntb + t, k * nhb + j))
    kern = functools.partial(_hgrn_prompt_kernel, layer=layer, hb=hb, tb=tb, L=L)
    return pl.pallas_call(
        kern,
        grid=(n_seq, nhb, ntb),
        in_specs=[sec(0), sec(1), sec(2), sec(3),
                  pl.BlockSpec((lb_logits.shape[0], w), lambda n, j, t: (0, j)),
                  pl.BlockSpec((1, w), lambda n, j, t: (0, j)),
                  pl.BlockSpec((1, hb, HG_DK, HG_DK), lambda n, j, t: (n, j, 0, 0)),
                  pl.BlockSpec(memory_space=pl.ANY)],
        out_specs=[pl.BlockSpec((tb, w), lambda n, j, t: (n * ntb + t, j)),
                   pl.BlockSpec((1, hb, HG_DK, HG_DK), lambda n, j, t: (n, j, 0, 0))],
        out_shape=[jax.ShapeDtypeStruct(z_prev.shape, BF16),
                   jax.ShapeDtypeStruct(s0.shape, F32)],
        scratch_shapes=[pltpu.VMEM((hb, HG_DK, HG_DK), F32)],
        input_output_aliases={7: 0},
        compiler_params=_params(("arbitrary", "arbitrary", "arbitrary")),
        name="hgrn_prompt",
    )(proj, proj, proj, proj, lb_logits, gnorm.reshape(1, d), s0, z_prev)


def _hgrn_step_kernel(q_ref, f_ref, v_ref, g_ref, lgt_ref, gn_ref, s_ref, zprev_ref, z_ref, so_ref,
                      *, layer, nb):
    lb = _hgrn_lower_bound(lgt_ref, layer)
    f = lb + (1.0 - lb) * jax.nn.sigmoid(f_ref[...])
    q = jax.nn.silu(q_ref[...])
    v = v_ref[...]
    f_t = f.T
    k_t = 1.0 - f_t
    qb = q.astype(BF16)
    outs = []
    for i in range(nb):
        s_new = f_t[:, i:i + 1] * s_ref[i, 0] + k_t[:, i:i + 1] * v[i:i + 1, :]
        so_ref[i, 0] = s_new
        outs.append(jnp.dot(qb[i:i + 1, :], s_new.astype(BF16), preferred_element_type=F32))
    o = jnp.concatenate(outs, axis=0)
    z_ref[...] = _hgrn_gate_out(o, gn_ref[...], g_ref[...]).astype(z_ref.dtype)


def hgrn_step(proj, z_prev, row_start, n_seq, s0, lb_logits, gnorm, layer):
    d = proj.shape[1] // 4
    heads = d // HG_DK
    nb = _pick(n_seq, (32, 16, 8))
    nnb = n_seq // nb
    rb = row_start // nb
    sec = lambda k: pl.BlockSpec((nb, HG_DK), lambda h, i: (rb + i, k * heads + h))
    s_spec = pl.BlockSpec((nb, 1, HG_DK, HG_DK), lambda h, i: (i, h, 0, 0))
    kern = functools.partial(_hgrn_step_kernel, layer=layer, nb=nb)
    return pl.pallas_call(
        kern,
        grid=(heads, nnb),
        in_specs=[sec(0), sec(1), sec(2), sec(3),
                  pl.BlockSpec((lb_logits.shape[0], HG_DK), lambda h, i: (0, h)),
                  pl.BlockSpec((1, HG_DK), lambda h, i: (0, h)),
                  s_spec,
                  pl.BlockSpec(memory_space=pl.ANY)],
        out_specs=[pl.BlockSpec((nb, HG_DK), lambda h, i: (rb + i, h)), s_spec],
        out_shape=[jax.ShapeDtypeStruct(z_prev.shape, BF16),
                   jax.ShapeDtypeStruct(s0.shape, F32)],
        input_output_aliases={7: 0},
        compiler_params=_params(("arbitrary", "arbitrary")),
        name="hgrn_step",
    )(proj, proj, proj, proj, lb_logits, gnorm.reshape(1, d), s0, z_prev)


def _row_tile(m, candidates):
    return _pick(m, candidates + (256, 128, 64, 32, 16))


def kernel(x_prompt, x_sample, state_s5_re, state_s5_im, state_conv, state_hgrn,
           norm_mix, norm_ffn, norm_final,
           s5_a_re, s5_a_im, s5_log_dt, s5_b_re, s5_b_im, s5_c_re, s5_c_im, s5_d, s5_w_glu,
           conv_w_in, conv_w, conv_w_out,
           hgrn_w_in, hgrn_lb_logits, hgrn_gnorm, hgrn_w_out,
           ffn_w_in, ffn_w_out):
    nb, t_len, d = x_prompt.shape
    ns = x_sample.shape[0]
    assert x_sample.shape[1] == 1
    mp = nb * t_len
    m = mp + ns
    depth = norm_mix.shape[0]
    g, p = s5_a_re.shape[1], s5_a_re.shape[2]
    gp = g * p
    dff = ffn_w_out.shape[1]
    heads = d // HG_DK
    tm_big = _row_tile(m, (1040,))
    tm_ffn = _row_tile(m, (832,))
    tn_big = _pick(d, (512, 256, 128))
    tnp_ff = _pick(dff, (256, 128))
    kk_ff = dff // 2 if (dff // 2) % LANES == 0 else dff

    assert N_MIXERS > 0
    h = None
    hb = None
    z_state = jnp.zeros((nb, gp), F32)

    p_re, p_im, s_re, s_im, p_conv, s_conv, p_hg, s_hg = [], [], [], [], [], [], [], []
    for layer in range(depth):
        kind, j = layer % N_MIXERS, layer // N_MIXERS
        if kind == 0:
            if layer == 0:
                h, u = stack_and_norm(x_prompt.reshape(mp, d), x_sample.reshape(ns, d), norm_mix[0])
            else:
                u = rmsnorm_rows(h, norm_mix[layer], 0, m, F32, lane_groups=True)
            prm = (s5_a_re[j], s5_a_im[j], s5_log_dt[j], s5_b_re[j], s5_b_im[j],
                   s5_c_re[j], s5_c_im[j], s5_d[j])
            zp, pre_, pim_ = s5_core(u, 0, nb, t_len, z_state, z_state, *prm)
            zs, sre_, sim_ = s5_core(u, mp, ns, 1, state_s5_re[j].reshape(ns, gp),
                                     state_s5_im[j].reshape(ns, gp), *prm)
            p_re.append(pre_)
            p_im.append(pim_)
            s_re.append(sre_)
            s_im.append(sim_)
            tn = _pick(d, (256, 128))
            h, hb = dense(planes_to_rows(zp, zs, BF16), [(s5_w_glu, j, 0, 0), (s5_w_glu, j, 0, d // tn)], d,
                          epi=_epi_glu_res, out_dtype=F32, res=h, emit_bf16=True, tm=tm_ffn, tn=2 * tn,
                          n_parts=2, name="s5_glu")
        elif kind == 1:
            proj = dense(hb, [(conv_w_in, j, 0, 0)], 3 * d, epi=_epi_plain, out_dtype=F32,
                         norm_w=norm_mix[layer], tm=tm_ffn, tn=2 * tn_big, n_parts=2, name="conv_in")
            z, pb = conv_prompt(proj, jnp.zeros((m, d), BF16), nb, t_len, jnp.zeros((nb, 2, d), F32),
                                conv_w[j])
            z, sb = conv_step(proj, z, mp, ns, state_conv[j], conv_w[j])
            p_conv.append(pb)
            s_conv.append(sb)
            h, hb = dense(z, [(conv_w_out, j, 0, 0)], d, epi=_epi_res, out_dtype=F32, res=h,
                          emit_bf16=True, tm=tm_big, tn=tn_big, name="conv_out")
        else:
            proj = dense(hb, [(hgrn_w_in, j, 0, 0)], 4 * d, epi=_epi_plain, out_dtype=F32,
                         norm_w=norm_mix[layer], tm=tm_ffn, tn=2 * tn_big, n_parts=2, name="hgrn_in")
            z, ps = hgrn_prompt(proj, jnp.zeros((m, d), BF16), nb, t_len,
                                jnp.zeros((nb, heads, HG_DK, HG_DK), F32),
                                hgrn_lb_logits, hgrn_gnorm[j], layer)
            z, ss = hgrn_step(proj, z, mp, ns, state_hgrn[j], hgrn_lb_logits, hgrn_gnorm[j], layer)
            p_hg.append(ps)
            s_hg.append(ss)
            h, hb = dense(z, [(hgrn_w_out, j, 0, 0)], d, epi=_epi_res, out_dtype=F32, res=h,
                          emit_bf16=True, tm=tm_big, tn=tn_big, name="hgrn_out")
        hf = dense(hb, [(ffn_w_in, layer, 0, 0), (ffn_w_in, layer, 0, dff // tnp_ff)], dff,
                   epi=_epi_swiglu, out_dtype=BF16, norm_w=norm_ffn[layer], tm=tm_big,
                   tn=2 * tnp_ff, n_parts=2, name="ffn_in")
        n_kb = dff // kk_ff
        for kb in range(n_kb):
            last = kb == n_kb - 1
            out = dense(hf, [(ffn_w_out, layer, kb, 0)], d, epi=_epi_res, out_dtype=F32, res=h,
                        emit_bf16=last, tm=tm_big, tn=tn_big, kk=kk_ff, x_kblock=kb, name="ffn_out")
            h, hb = out if last else (out, None)

    y_prompt = rmsnorm_rows(h, norm_final, 0, mp, F32).reshape(nb, t_len, d)
    y_sample = rmsnorm_rows(h, norm_final, mp, ns, F32).reshape(ns, 1, d)
    return (y_prompt, y_sample,
            jnp.stack(p_re).reshape(-1, nb, g, p), jnp.stack(p_im).reshape(-1, nb, g, p),
            jnp.stack(p_conv), jnp.stack(p_hg),
            jnp.stack(s_re).reshape(-1, ns, g, p), jnp.stack(s_im).reshape(-1, ns, g, p),
            jnp.stack(s_conv), jnp.stack(s_hg))
```
